```python
import jax, jax.numpy as jnp
from jax import lax
import numpy as np

D_MODEL = 1024
BATCH = 4
SEQ = 8192
DEPTH = 2

GRID_W = 64
CTX_LEN = 256
N_MIXERS = 2
EPS = 1e-6
MLA_HEADS = 8
MLA_Q_RANK = 512
MLA_KV_RANK = 256
MLA_NOPE = 128
MLA_ROPE = 64
MLA_V = 128
Q_BLOCK = 128
ROPE_BASE = 10000.0
GLA_HEADS = 4
GLA_DK = D_MODEL // 2 // GLA_HEADS
GLA_DV = D_MODEL // GLA_HEADS
GLA_GATE_RANK = 16
GLA_GATE_NORM = 16.0
GLA_CHUNK = 64
N_EXPERTS = 32
TOP_K = 4
D_FF = D_MODEL
SWIGLU_ALPHA = 1.702
SWIGLU_LIMIT = 7.0
MOE_BLOCK = 128

kernel_name = 'hybrid_mla_gla_moe_dit_trunk'


def rmsnorm(x, g):
    xf = x.astype(jnp.float32)
    y = xf * lax.rsqrt(jnp.mean(xf * xf, axis=-1, keepdims=True) + EPS)
    return (y * g.astype(jnp.float32)).astype(x.dtype)


def axial_rope_tables(n):
    rows = n // GRID_W
    row = jnp.repeat(jnp.arange(rows, dtype=jnp.float32), GRID_W)
    col = jnp.tile(jnp.arange(GRID_W, dtype=jnp.float32), rows)
    nf = MLA_ROPE // 4
    inv = jnp.power(ROPE_BASE, -jnp.arange(nf, dtype=jnp.float32) / nf)
    ang = jnp.concatenate([row[:, None] * inv, col[:, None] * inv], axis=-1)
    return jnp.cos(ang), jnp.sin(ang)


def apply_axial_rope(x, cos, sin):
    nf = MLA_ROPE // 4
    shp = (cos.shape[0],) + (1,) * (x.ndim - 3) + (2, nf)
    c = cos.reshape(shp)
    s = sin.reshape(shp)
    xs = x.astype(jnp.float32).reshape(x.shape[:-1] + (2, 2, nf))
    x1, x2 = xs[..., 0, :], xs[..., 1, :]
    out = jnp.stack([x1 * c - x2 * s, x2 * c + x1 * s], axis=-2)
    return out.reshape(x.shape).astype(x.dtype)


def mla_mixer(h, hc, cos, sin, w_in, q_norm, w_uq, kv_norm, w_ukv, w_o, with_ctx_out):
    H = MLA_HEADS
    w_uk = w_ukv[:, :H * MLA_NOPE].reshape(MLA_KV_RANK, H, MLA_NOPE)
    w_uv = w_ukv[:, H * MLA_NOPE:].reshape(MLA_KV_RANK, H, MLA_V)
    scale = (MLA_NOPE + MLA_ROPE) ** -0.5

    def project(z, rotate):
        b, l, _ = z.shape
        a = z @ w_in
        cq = rmsnorm(a[..., :MLA_Q_RANK], q_norm)
        ckv = rmsnorm(a[..., MLA_Q_RANK:MLA_Q_RANK + MLA_KV_RANK], kv_norm)
        kr = a[..., MLA_Q_RANK + MLA_KV_RANK:]
        q = (cq @ w_uq).reshape(b, l, H, MLA_NOPE + MLA_ROPE)
        q_nope, q_rope = q[..., :MLA_NOPE], q[..., MLA_NOPE:]
        if rotate:
            q_rope = apply_axial_rope(q_rope, cos, sin)
            kr = apply_axial_rope(kr, cos, sin)
        q_eff = jnp.concatenate([jnp.einsum('blhn,chn->blhc', q_nope, w_uk), q_rope], axis=-1)
        keys = jnp.concatenate([ckv, kr], axis=-1)
        return q_eff, keys, ckv

    def attend(q, keys, vals):
        s = jnp.einsum('bqhc,bkc->bhqk', q, keys).astype(jnp.float32) * scale
        p = jax.nn.softmax(s, axis=-1).astype(vals.dtype)
        return jnp.einsum('bhqk,bkc->bqhc', p, vals)

    def out_proj(o):
        b, l = o.shape[:2]
        return jnp.einsum('blhc,chv->blhv', o, w_uv).reshape(b, l, H * MLA_V) @ w_o

    q_l, k_l, v_l = project(h, True)
    q_c, k_c, v_c = project(hc, False)
    k_all = jnp.concatenate([k_l, k_c], axis=1)
    v_all = jnp.concatenate([v_l, v_c], axis=1)
    B, N = h.shape[:2]
    nb = N // Q_BLOCK
    q_blocks = q_l.reshape(B, nb, Q_BLOCK, H, MLA_KV_RANK + MLA_ROPE).transpose(1, 0, 2, 3, 4)
    o = lax.map(lambda qb: attend(qb, k_all, v_all), q_blocks)
    o = o.transpose(1, 0, 2, 3, 4).reshape(B, N, H, MLA_KV_RANK)
    y = out_proj(o)
    yc = out_proj(attend(q_c, k_c, v_c)) if with_ctx_out else None
    return y, yc


def gla_scan(q, k, v, g, s0):
    B, L, H, DK = q.shape
    DV = v.shape[-1]
    C = GLA_CHUNK
    nc = L // C
    mask = jnp.tril(jnp.ones((C, C), dtype=bool))

    def chunks(t):
        return t.reshape(B, nc, C, H, t.shape[-1]).transpose(1, 0, 3, 2, 4).astype(jnp.float32)

    def step(S, inp):
        qc, kc, vc, gc = inp
        b = jnp.cumsum(gc, axis=2)
        o_inter = jnp.einsum('bhcd,bhde->bhce', qc * jnp.exp(b), S)
        diff = b[:, :, :, None, :] - b[:, :, None, :, :]
        decay = jnp.exp(jnp.where(mask[:, :, None], diff, -jnp.inf))
        A = jnp.einsum('bhid,bhijd,bhjd->bhij', qc, decay, kc)
        o_intra = jnp.einsum('bhij,bhje->bhie', A, vc)
        b_last = b[:, :, -1:, :]
        S_new = jnp.exp(b_last[:, :, 0, :])[..., None] * S + jnp.einsum('bhjd,bhje->bhde', kc * jnp.exp(b_last - b), vc)
        return S_new, o_inter + o_intra

    s_fin, o = lax.scan(step, s0, (chunks(q), chunks(k), chunks(v), chunks(g)))
    o = o.transpose(1, 0, 3, 2, 4).reshape(B, L, H, DV).astype(v.dtype)
    return o, s_fin


def gla_mixer(h, hc, w_in, w_gate1, w_gate2, b_gate, head_norm, w_o, with_ctx_out):
    H = GLA_HEADS
    nqk = H * GLA_DK
    nv = H * GLA_DV

    def project(z):
        b, l, _ = z.shape
        a = z @ w_in
        q = a[..., :nqk].reshape(b, l, H, GLA_DK) * (GLA_DK ** -0.5)
        k = a[..., nqk:2 * nqk].reshape(b, l, H, GLA_DK)
        v = a[..., 2 * nqk:2 * nqk + nv].reshape(b, l, H, GLA_DV)
        r = a[..., 2 * nqk + nv:]
        g_f = (jax.nn.log_sigmoid((z @ w_gate1[0]) @ w_gate2[0] + b_gate[0]) / GLA_GATE_NORM).reshape(b, l, H, GLA_DK)
        g_b = (jax.nn.log_sigmoid((z @ w_gate1[1]) @ w_gate2[1] + b_gate[1]) / GLA_GATE_NORM).reshape(b, l, H, GLA_DK)
        return q, k, v, r, g_f, g_b

    def flip(t):
        return jnp.flip(t, axis=1)

    def finish(o, r):
        b, l = o.shape[:2]
        o = rmsnorm(o, head_norm).reshape(b, l, nv)
        return (o * jax.nn.silu(r)) @ w_o

    B = h.shape[0]
    s0 = jnp.zeros((B, H, GLA_DK, GLA_DV), jnp.float32)
    qc, kc, vc, rc, gcf, gcb = project(hc)
    oc_f, sc_f = gla_scan(qc, kc, vc, gcf, s0)
    oc_b, sc_b = gla_scan(flip(qc), flip(kc), flip(vc), flip(gcb), s0)
    q, k, v, r, g_f, g_b = project(h)
    o_f, _ = gla_scan(q, k, v, g_f, sc_f)
    o_b, _ = gla_scan(flip(q), flip(k), flip(v), flip(g_b), sc_b)
    y = finish(o_f + flip(o_b), r)
    yc = finish(oc_f + flip(oc_b), rc) if with_ctx_out else None
    return y, yc


def moe(t, w_router, b_router, w_gu, b_gu, w_down, b_down):
    T, D = t.shape
    logits = (t @ w_router + b_router).astype(jnp.float32)
    top_val, top_idx = lax.top_k(logits, TOP_K)
    gate = jax.nn.softmax(top_val, axis=-1)
    n_assign = T * TOP_K
    flat_e = top_idx.reshape(-1)
    flat_tok = jnp.repeat(jnp.arange(T, dtype=jnp.int32), TOP_K)
    flat_w = gate.reshape(-1)
    order = jnp.argsort(flat_e)
    se = flat_e[order]
    counts = jnp.bincount(flat_e, length=N_EXPERTS)
    padded = (counts + MOE_BLOCK - 1) // MOE_BLOCK * MOE_BLOCK
    start_s = jnp.cumsum(counts) - counts
    end_p = jnp.cumsum(padded)
    start_p = end_p - padded
    dest = start_p[se] + jnp.arange(n_assign) - start_s[se]
    n_rows = (n_assign + MOE_BLOCK - 1) // MOE_BLOCK * MOE_BLOCK + N_EXPERTS * MOE_BLOCK
    n_blocks = n_rows // MOE_BLOCK
    row_tok = jnp.zeros((n_rows,), jnp.int32).at[dest].set(flat_tok[order])
    row_w = jnp.zeros((n_rows,), jnp.float32).at[dest].set(flat_w[order])
    block_e = jnp.minimum(jnp.searchsorted(end_p, jnp.arange(n_blocks) * MOE_BLOCK, side='right'), N_EXPERTS - 1)

    def block_fn(args):
        tok, w, e = args
        xb = t[tok]
        gu = xb @ w_gu[e] + b_gu[e]
        g_, u_ = gu[:, ::2], gu[:, 1::2]
        g_ = jnp.minimum(g_, SWIGLU_LIMIT)
        u_ = jnp.clip(u_, -SWIGLU_LIMIT, SWIGLU_LIMIT)
        y = (u_ + 1) * (g_ * jax.nn.sigmoid(SWIGLU_ALPHA * g_))
        y = y @ w_down[e] + b_down[e]
        return y * w[:, None].astype(y.dtype)

    rows = lax.map(block_fn, (row_tok.reshape(n_blocks, MOE_BLOCK), row_w.reshape(n_blocks, MOE_BLOCK), block_e))
    return jax.ops.segment_sum(rows.reshape(n_rows, D), row_tok, num_segments=T)


def setup_inputs(seed: int = 0) -> dict:
    key = jax.random.key(seed)
    ks = jax.random.split(key, 32)
    D = D_MODEL
    n_mla = (DEPTH + 1) // 2
    n_gla = DEPTH // 2
    mla_in = MLA_Q_RANK + MLA_KV_RANK + MLA_ROPE
    gla_in = 2 * GLA_HEADS * GLA_DK + 2 * GLA_HEADS * GLA_DV

    def nrm(k, shape, scale):
        return jax.random.normal(k, shape, jnp.float32) * scale

    return {
        'x': nrm(ks[0], (BATCH, SEQ, D), 1.0),
        'c': nrm(ks[1], (BATCH, D), 1.0),
        'ctx': nrm(ks[2], (BATCH, CTX_LEN, D), 1.0),
        'c_ctx': nrm(ks[3], (D,), 1.0),
        'mod_w': nrm(ks[4], (DEPTH, D, 6 * D), 0.3 * D ** -0.5),
        'mod_b': nrm(ks[5], (DEPTH, 6 * D), 0.02),
        'norm1': 1.0 + nrm(ks[6], (DEPTH, D), 0.05),
        'norm2': 1.0 + nrm(ks[7], (DEPTH, D), 0.05),
        'mla_w_in': nrm(ks[8], (n_mla, D, mla_in), D ** -0.5),
        'mla_q_norm': 1.0 + nrm(ks[9], (n_mla, MLA_Q_RANK), 0.05),
        'mla_w_uq': nrm(ks[10], (n_mla, MLA_Q_RANK, MLA_HEADS * (MLA_NOPE + MLA_ROPE)), MLA_Q_RANK ** -0.5),
        'mla_kv_norm': 1.0 + nrm(ks[11], (n_mla, MLA_KV_RANK), 0.05),
        'mla_w_ukv': nrm(ks[12], (n_mla, MLA_KV_RANK, MLA_HEADS * (MLA_NOPE + MLA_V)), MLA_KV_RANK ** -0.5),
        'mla_w_o': nrm(ks[13], (n_mla, MLA_HEADS * MLA_V, D), (MLA_HEADS * MLA_V) ** -0.5),
        'gla_w_in': nrm(ks[14], (n_gla, D, gla_in), D ** -0.5),
        'gla_w_gate1': nrm(ks[15], (n_gla, 2, D, GLA_GATE_RANK), D ** -0.5),
        'gla_w_gate2': nrm(ks[16], (n_gla, 2, GLA_GATE_RANK, GLA_HEADS * GLA_DK), GLA_GATE_RANK ** -0.5),
        'gla_b_gate': nrm(ks[17], (n_gla, 2, GLA_HEADS * GLA_DK), 0.1),
        'gla_head_norm': 1.0 + nrm(ks[18], (n_gla, GLA_DV), 0.05),
        'gla_w_o': nrm(ks[19], (n_gla, GLA_HEADS * GLA_DV, D), (GLA_HEADS * GLA_DV) ** -0.5),
        'moe_w_router': nrm(ks[20], (DEPTH, D, N_EXPERTS), D ** -0.5),
        'moe_b_router': nrm(ks[21], (DEPTH, N_EXPERTS), 0.01),
        'moe_w_gu': nrm(ks[22], (DEPTH, N_EXPERTS, D, 2 * D_FF), D ** -0.5),
        'moe_b_gu': nrm(ks[23], (DEPTH, N_EXPERTS, 2 * D_FF), 0.02),
        'moe_w_down': nrm(ks[24], (DEPTH, N_EXPERTS, D_FF, D), D_FF ** -0.5),
        'moe_b_down': nrm(ks[25], (DEPTH, N_EXPERTS, D), 0.02),
        'final_norm': 1.0 + nrm(ks[26], (D,), 0.05),
    }


def reference(x, c, ctx, c_ctx, mod_w, mod_b, norm1, norm2, mla_w_in, mla_q_norm, mla_w_uq, mla_kv_norm, mla_w_ukv, mla_w_o,
              gla_w_in, gla_w_gate1, gla_w_gate2, gla_b_gate, gla_head_norm, gla_w_o,
              moe_w_router, moe_b_router, moe_w_gu, moe_b_gu, moe_w_down, moe_b_down, final_norm):
    B, N, D = x.shape
    Lc = ctx.shape[1]
    cos, sin = axial_rope_tables(N)
    xc = ctx
    for i in range(DEPTH):
        last = i == DEPTH - 1
        mod = (jax.nn.silu(c) @ mod_w[i] + mod_b[i])[:, None, :]
        mod_c = (jax.nn.silu(c_ctx) @ mod_w[i] + mod_b[i])[None, None, :]
        sh1, sc1, g1, sh2, sc2, g2 = jnp.split(mod, 6, axis=-1)
        sh1c, sc1c, g1c, sh2c, sc2c, g2c = jnp.split(mod_c, 6, axis=-1)
        h = rmsnorm(x, norm1[i]) * (1 + sc1) + sh1
        hc = rmsnorm(xc, norm1[i]) * (1 + sc1c) + sh1c
        j = i // N_MIXERS
        if i % N_MIXERS == 0:
            y, yc = mla_mixer(h, hc, cos, sin, mla_w_in[j], mla_q_norm[j], mla_w_uq[j], mla_kv_norm[j], mla_w_ukv[j], mla_w_o[j], not last)
        else:
            y, yc = gla_mixer(h, hc, gla_w_in[j], gla_w_gate1[j], gla_w_gate2[j], gla_b_gate[j], gla_head_norm[j], gla_w_o[j], not last)
        x = x + g1 * y
        h = rmsnorm(x, norm2[i]) * (1 + sc2) + sh2
        if last:
            y = moe(h.reshape(B * N, D), moe_w_router[i], moe_b_router[i], moe_w_gu[i], moe_b_gu[i], moe_w_down[i], moe_b_down[i]).reshape(B, N, D)
            x = x + g2 * y
        else:
            xc = xc + g1c * yc
            hc = rmsnorm(xc, norm2[i]) * (1 + sc2c) + sh2c
            tokens = jnp.concatenate([h.reshape(B * N, D), hc.reshape(B * Lc, D)], axis=0)
            out = moe(tokens, moe_w_router[i], moe_b_router[i], moe_w_gu[i], moe_b_gu[i], moe_w_down[i], moe_b_down[i])
            x = x + g2 * out[:B * N].reshape(B, N, D)
            xc = xc + g2c * out[B * N:].reshape(B, Lc, D)
    return rmsnorm(x, final_norm)
```

```python
import functools

import numpy as np
import jax
import jax.numpy as jnp
from jax import lax
from jax.experimental import pallas as pl
from jax.experimental.pallas import tpu as pltpu

F32 = jnp.float32
BF16 = jnp.bfloat16
EPS = 1e-6

GRID_W = 64
ROPE_BASE = 10000.0
MLA_HEADS = 8
MLA_Q_RANK = 512
MLA_KV_RANK = 256
MLA_NOPE = 128
MLA_ROPE = 64
MLA_V = 128
MLA_KDIM = MLA_KV_RANK + MLA_ROPE
GLA_HEADS = 4
GLA_GATE_RANK = 16
GLA_GATE_NORM = 16.0
GLA_CHUNK = 128
N_EXPERTS = 32
TOP_K = 4
SWIGLU_ALPHA = 1.702
SWIGLU_LIMIT = 7.0

VMEM_LIMIT_BYTES = 56 * 1024 * 1024
MOE_ROWS = 512
TOKEN_TILE = 512
ATTN_Q_TILE = 128
ATTN_KV_TILE = 768
ROUTER_TILE = 512
COPY_TILE = 256
GLA_BLOCK = 256

_HI = lax.Precision.HIGHEST
_NT = (((1,), (1,)), ((), ()))
_TN = (((0,), (0,)), ((), ()))


def _cparams(*sem):
    return pltpu.CompilerParams(dimension_semantics=sem, vmem_limit_bytes=VMEM_LIMIT_BYTES)


def _rms(x, g):
    return x * lax.rsqrt(jnp.mean(x * x, axis=-1, keepdims=True) + EPS) * g


def _modrow(mod_ref, k):
    return mod_ref[0, pl.ds(k, 1), :]


def _mod_kernel(cc_ref, w_ref, b_ref, o_ref):
    cc = cc_ref[...]
    s = cc * jax.nn.sigmoid(cc)
    o_ref[0] = jnp.dot(s, w_ref[0], preferred_element_type=F32, precision=_HI) + b_ref[0]


def _modulation(cc, mod_w, mod_b):
    depth, d, d6 = mod_w.shape
    tn = 1536
    return pl.pallas_call(
        _mod_kernel,
        grid=(depth, d6 // tn),
        in_specs=[
            pl.BlockSpec((8, d), lambda i, j: (0, 0)),
            pl.BlockSpec((1, d, tn), lambda i, j: (i, 0, j)),
            pl.BlockSpec((1, 1, tn), lambda i, j: (i, 0, j)),
        ],
        out_specs=pl.BlockSpec((1, 8, tn), lambda i, j: (i, 0, j)),
        out_shape=jax.ShapeDtypeStruct((depth, 8, d6), F32),
        compiler_params=_cparams("parallel", "parallel"),
        name="modulation",
    )(cc, mod_w, mod_b.reshape(depth, 1, d6))


def _mla_proj_kernel(x_ref, mod_ref, n1_ref, win_ref, qn_ref, kvn_ref, wqn_ref, wqr_ref, wqs_ref, wuk_ref,
                     cos_ref, sin_ref, q_ref, k_ref, *, scale):
    x = x_ref[0]
    h = _rms(x, n1_ref[...]) * (1.0 + _modrow(mod_ref, 1)) + _modrow(mod_ref, 0)
    a = jnp.dot(h.astype(BF16), win_ref[...], preferred_element_type=F32)
    cq = _rms(a[:, :MLA_Q_RANK], qn_ref[...]).astype(BF16)
    ckv = _rms(a[:, MLA_Q_RANK:MLA_Q_RANK + MLA_KV_RANK], kvn_ref[...])
    cos = cos_ref[...]
    sin = sin_ref[...]
    o = MLA_Q_RANK + MLA_KV_RANK
    kr = a[:, o:o + MLA_ROPE] * cos[:, :MLA_ROPE] + a[:, o + MLA_ROPE:o + 2 * MLA_ROPE] * sin[:, :MLA_ROPE]
    k_ref[0, :, :MLA_KV_RANK] = ckv.astype(BF16)
    k_ref[0, :, MLA_KV_RANK:] = kr.astype(BF16)
    qn = jnp.dot(cq, wqn_ref[...], preferred_element_type=F32)
    qr = (jnp.dot(cq, wqr_ref[...], preferred_element_type=F32) * cos
          + jnp.dot(cq, wqs_ref[...], preferred_element_type=F32) * sin)
    for hh in range(MLA_HEADS):
        qe = jnp.dot(qn[:, hh * MLA_NOPE:(hh + 1) * MLA_NOPE].astype(BF16), wuk_ref[hh], preferred_element_type=F32)
        q_ref[0, hh, :, :MLA_KV_RANK] = (qe * scale).astype(BF16)
        q_ref[0, hh, :, MLA_KV_RANK:] = (qr[:, hh * MLA_ROPE:(hh + 1) * MLA_ROPE] * scale).astype(BF16)


def _mla_project(x, mod, mod_row, n1, w, cos, sin, tile):
    b, l, d = x.shape
    tile = min(tile, l)
    scale = float((MLA_NOPE + MLA_ROPE) ** -0.5)
    const = lambda *shape: pl.BlockSpec(shape, lambda i, j: (0,) * len(shape))
    return pl.pallas_call(
        functools.partial(_mla_proj_kernel, scale=scale),
        grid=(b, l // tile),
        in_specs=[
            pl.BlockSpec((1, tile, d), lambda i, j: (i, j, 0)),
            pl.BlockSpec((1, 6, d), lambda i, j: (mod_row(i), 0, 0)),
            const(1, d),
            const(*w["w_in"].shape),
            const(1, MLA_Q_RANK),
            const(1, MLA_KV_RANK),
            const(*w["w_qn"].shape),
            const(*w["w_qr"].shape),
            const(*w["w_qs"].shape),
            const(*w["w_ukT"].shape),
            pl.BlockSpec((tile, MLA_HEADS * MLA_ROPE), lambda i, j: (j, 0)),
            pl.BlockSpec((tile, MLA_HEADS * MLA_ROPE), lambda i, j: (j, 0)),
        ],
        out_specs=[
            pl.BlockSpec((1, MLA_HEADS, tile, MLA_KDIM), lambda i, j: (i, 0, j, 0)),
            pl.BlockSpec((1, tile, MLA_KDIM), lambda i, j: (i, j, 0)),
        ],
        out_shape=[
            jax.ShapeDtypeStruct((b, MLA_HEADS, l, MLA_KDIM), BF16),
            jax.ShapeDtypeStruct((b, l, MLA_KDIM), BF16),
        ],
        compiler_params=_cparams("parallel", "parallel"),
        name="mla_project",
    )(x, mod, n1, w["w_in"], w["q_norm"], w["kv_norm"], w["w_qn"], w["w_qr"], w["w_qs"], w["w_ukT"], cos, sin)


def _attn_kernel(q_ref, k_ref, x_ref, mod_ref, wuv_ref, wo_ref, n2_ref, xo_ref, h2_ref, m_sc, l_sc, acc_sc,
                 *, tq, tk, nkv):
    q = q_ref[0].reshape(MLA_HEADS * tq, MLA_KDIM)
    m_sc[...] = jnp.full(m_sc.shape, -jnp.inf, F32)
    l_sc[...] = jnp.zeros(l_sc.shape, F32)
    acc_sc[...] = jnp.zeros(acc_sc.shape, F32)

    def body(c, carry):
        start = pl.multiple_of(c * tk, tk)
        k = k_ref[0, pl.ds(start, tk), :]
        s = lax.dot_general(q, k, _NT, preferred_element_type=F32)
        m_prev = m_sc[...]
        m_new = jnp.maximum(m_prev, jnp.max(s, axis=-1, keepdims=True))
        alpha = jnp.exp(m_prev - m_new)
        p = jnp.exp(s - m_new)
        l_sc[...] = alpha * l_sc[...] + jnp.sum(p, axis=-1, keepdims=True)
        acc_sc[...] = alpha * acc_sc[...] + jnp.dot(p.astype(BF16), k[:, :MLA_KV_RANK], preferred_element_type=F32)
        m_sc[...] = m_new
        return carry

    lax.fori_loop(0, nkv, body, 0)
    o = (acc_sc[...] / l_sc[...]).astype(BF16)
    u = [jnp.dot(o[hh * tq:(hh + 1) * tq], wuv_ref[hh], preferred_element_type=F32) for hh in range(MLA_HEADS)]
    u = jnp.concatenate(u, axis=-1).astype(BF16)
    y = jnp.dot(u, wo_ref[...], preferred_element_type=F32)
    xn = x_ref[0] + _modrow(mod_ref, 2) * y
    xo_ref[0] = xn
    h2_ref[0] = _rms(xn, n2_ref[...]) * (1.0 + _modrow(mod_ref, 4)) + _modrow(mod_ref, 3)


def _mla_attend(q, keys, x, mod, mod_row, n2, w, tq, tk):
    b, l, d = x.shape
    lk = keys.shape[1]
    tq = min(tq, l)
    tk = min(tk, lk)
    assert lk % tk == 0 and l % tq == 0
    rows = MLA_HEADS * tq
    const = lambda *shape: pl.BlockSpec(shape, lambda i, j: (0,) * len(shape))
    return pl.pallas_call(
        functools.partial(_attn_kernel, tq=tq, tk=tk, nkv=lk // tk),
        grid=(b, l // tq),
        in_specs=[
            pl.BlockSpec((1, MLA_HEADS, tq, MLA_KDIM), lambda i, j: (i, 0, j, 0)),
            pl.BlockSpec((1, lk, MLA_KDIM), lambda i, j: (i, 0, 0)),
            pl.BlockSpec((1, tq, d), lambda i, j: (i, j, 0)),
            pl.BlockSpec((1, 6, d), lambda i, j: (mod_row(i), 0, 0)),
            const(*w["w_uv"].shape),
            const(*w["w_o"].shape),
            const(1, d),
        ],
        out_specs=[
            pl.BlockSpec((1, tq, d), lambda i, j: (i, j, 0)),
            pl.BlockSpec((1, tq, d), lambda i, j: (i, j, 0)),
        ],
        out_shape=[jax.ShapeDtypeStruct((b, l, d), F32), jax.ShapeDtypeStruct((b, l, d), F32)],
        scratch_shapes=[
            pltpu.VMEM((rows, 1), F32),
            pltpu.VMEM((rows, 1), F32),
            pltpu.VMEM((rows, MLA_KV_RANK), F32),
        ],
        compiler_params=_cparams("parallel", "arbitrary"),
        name="mla_attend",
    )(q, keys, x, mod, w["w_uv"], w["w_o"], n2)


def _router_kernel(h_ref, wt_ref, b_ref, idx_ref, gate_ref, rank_ref, cnt_ref, carry_sc, *, tile):
    @pl.when(pl.program_id(0) == 0)
    def _():
        carry_sc[...] = jnp.zeros(carry_sc.shape, F32)

    logits = lax.dot_general(wt_ref[...], h_ref[...], _NT, preferred_element_type=F32, precision=_HI) + b_ref[...]
    eid = lax.broadcasted_iota(jnp.int32, logits.shape, 0).astype(F32)
    vals, hots = [], []
    cur = logits
    for k in range(TOP_K):
        mx = jnp.max(cur, axis=0, keepdims=True)
        first = jnp.min(jnp.where(cur == mx, eid, float(N_EXPERTS)), axis=0, keepdims=True)
        hot = eid == first
        idx_ref[pl.ds(k, 1), :] = first.astype(jnp.int32)
        vals.append(mx)
        hots.append(hot)
        cur = jnp.where(hot, -jnp.inf, cur)
    ex = [jnp.exp(v - vals[0]) for v in vals]
    den = ex[0] + ex[1] + ex[2] + ex[3]
    for k in range(TOP_K):
        gate_ref[pl.ds(k, 1), :] = ex[k] / den
    cnt = sum(jnp.where(hot, 1.0, 0.0) for hot in hots)
    r = lax.broadcasted_iota(jnp.int32, (tile, tile), 0)
    c = lax.broadcasted_iota(jnp.int32, (tile, tile), 1)
    upper = jnp.where(r <= c, 1.0, 0.0).astype(BF16)
    incl = jnp.dot(cnt.astype(BF16), upper, preferred_element_type=F32)
    pos = carry_sc[...] + incl - 1.0
    for k in range(TOP_K):
        rank_ref[pl.ds(k, 1), :] = jnp.sum(jnp.where(hots[k], pos, 0.0), axis=0, keepdims=True).astype(jnp.int32)
    carry_sc[...] = carry_sc[...] + jnp.sum(cnt, axis=1, keepdims=True)
    cnt_ref[...] = jnp.broadcast_to(carry_sc[...], cnt_ref.shape)


def _route(h, w_router, b_router, tile):
    t, d = h.shape
    tile = min(tile, t)
    assert t % tile == 0
    tok = lambda i: (0, i)
    idx, gate, rank, cnt = pl.pallas_call(
        functools.partial(_router_kernel, tile=tile),
        grid=(t // tile,),
        in_specs=[
            pl.BlockSpec((tile, d), lambda i: (i, 0)),
            pl.BlockSpec((N_EXPERTS, d), lambda i: (0, 0)),
            pl.BlockSpec((N_EXPERTS, 1), lambda i: (0, 0)),
        ],
        out_specs=[
            pl.BlockSpec((TOP_K, tile), tok),
            pl.BlockSpec((TOP_K, tile), tok),
            pl.BlockSpec((TOP_K, tile), tok),
            pl.BlockSpec((N_EXPERTS, 128), lambda i: (0, 0)),
        ],
        out_shape=[
            jax.ShapeDtypeStruct((TOP_K, t), jnp.int32),
            jax.ShapeDtypeStruct((TOP_K, t), F32),
            jax.ShapeDtypeStruct((TOP_K, t), jnp.int32),
            jax.ShapeDtypeStruct((N_EXPERTS, 128), F32),
        ],
        scratch_shapes=[pltpu.VMEM((N_EXPERTS, 1), F32)],
        compiler_params=_cparams("arbitrary"),
        name="moe_router",
    )(h, w_router.T, b_router.reshape(N_EXPERTS, 1))
    return idx, gate, rank, cnt[:, 0].astype(jnp.int32)


def _row_copy(src_hbm, s, dst_ref, d, sem):
    return pltpu.make_async_copy(src_hbm.at[pl.ds(s, 1), :], dst_ref.at[pl.ds(d, 1), :], sem)


def _dispatch_kernel(dest_hbm, h_hbm, xs_in, xs_hbm, dsm, isem, sem, *, tile):
    del xs_in
    i = pl.program_id(0)
    n = TOP_K * tile
    cp = pltpu.make_async_copy(dest_hbm.at[pl.ds(pl.multiple_of(i * n, n), n)], dsm, isem)
    cp.start()
    cp.wait()

    def issue(j, carry):
        for k in range(TOP_K):
            _row_copy(h_hbm, i * tile + j, xs_hbm, dsm[k * tile + j], sem).start()
        return carry

    lax.fori_loop(0, tile, issue, 0)

    def drain(j, carry):
        _row_copy(h_hbm, 0, xs_hbm, 0, sem).wait()
        return carry

    lax.fori_loop(0, n, drain, 0)


def _dispatch(dest_tiles, h, n_rows, tile):
    t, d = h.shape
    xs0 = jnp.zeros((n_rows, d), F32)
    return pl.pallas_call(
        functools.partial(_dispatch_kernel, tile=tile),
        grid=(t // tile,),
        in_specs=[pl.BlockSpec(memory_space=pl.ANY)] * 3,
        out_specs=pl.BlockSpec(memory_space=pl.ANY),
        out_shape=jax.ShapeDtypeStruct((n_rows, d), F32),
        scratch_shapes=[pltpu.SMEM((TOP_K * tile,), jnp.int32), pltpu.SemaphoreType.DMA, pltpu.SemaphoreType.DMA],
        input_output_aliases={2: 0},
        compiler_params=_cparams("arbitrary"),
        name="moe_dispatch",
    )(dest_tiles, h, xs0)


def _ffn_kernel(be_ref, nb_ref, xs_ref, wg_ref, wu_ref, bg_ref, bu_ref, wd_ref, bd_ref, y_ref):
    i = pl.program_id(0)

    @pl.when(i < nb_ref[0])
    def _():
        xb = xs_ref[...].astype(BF16)
        g = jnp.dot(xb, wg_ref[0], preferred_element_type=F32) + bg_ref[0]
        u = jnp.dot(xb, wu_ref[0], preferred_element_type=F32) + bu_ref[0]
        g = jnp.minimum(g, SWIGLU_LIMIT)
        u = jnp.clip(u, -SWIGLU_LIMIT, SWIGLU_LIMIT)
        a = (u + 1.0) * (g * jax.nn.sigmoid(SWIGLU_ALPHA * g))
        y_ref[...] = jnp.dot(a.astype(BF16), wd_ref[0], preferred_element_type=F32) + bd_ref[0]

    @pl.when(i >= nb_ref[0])
    def _():
        y_ref[...] = jnp.zeros(y_ref.shape, F32)


def _expert_ffn(block_e, n_used, xs, w, rows):
    n_rows, d = xs.shape
    f = w["w_g"].shape[2]
    ex = lambda i, be, nb: (be[i], 0, 0)
    return pl.pallas_call(
        _ffn_kernel,
        grid_spec=pltpu.PrefetchScalarGridSpec(
            num_scalar_prefetch=2,
            grid=(n_rows // rows,),
            in_specs=[
                pl.BlockSpec((rows, d), lambda i, be, nb: (i, 0)),
                pl.BlockSpec((1, d, f), ex),
                pl.BlockSpec((1, d, f), ex),
                pl.BlockSpec((1, 1, f), ex),
                pl.BlockSpec((1, 1, f), ex),
                pl.BlockSpec((1, f, d), ex),
                pl.BlockSpec((1, 1, d), ex),
            ],
            out_specs=pl.BlockSpec((rows, d), lambda i, be, nb: (i, 0)),
        ),
        out_shape=jax.ShapeDtypeStruct((n_rows, d), F32),
        compiler_params=_cparams("arbitrary"),
        name="moe_ffn",
    )(block_e, n_used, xs, w["w_g"], w["w_u"], w["b_g"], w["b_u"], w["w_d"], w["b_d"])


def _combine_kernel(dest_hbm, y_hbm, x_ref, gate_ref, mod_ref, fn_ref, o_ref, dsm, yg, isem, sem, *, tile, final):
    i = pl.program_id(0)
    n = TOP_K * tile
    cp = pltpu.make_async_copy(dest_hbm.at[pl.ds(pl.multiple_of(i * n, n), n)], dsm, isem)
    cp.start()
    cp.wait()

    def issue(j, carry):
        for k in range(TOP_K):
            _row_copy(y_hbm, dsm[k * tile + j], yg.at[k], j, sem).start()
        return carry

    lax.fori_loop(0, tile, issue, 0)

    def drain(j, carry):
        _row_copy(y_hbm, 0, yg.at[0], 0, sem).wait()
        return carry

    lax.fori_loop(0, n, drain, 0)
    gate = gate_ref[...]
    acc = yg[0] * gate[:, 0:1]
    for k in range(1, TOP_K):
        acc = acc + yg[k] * gate[:, k:k + 1]
    xn = x_ref[...] + _modrow(mod_ref, 5) * acc
    o_ref[...] = _rms(xn, fn_ref[...]) if final else xn


def _combine(dest_tiles, y, x, gate_t, mod, mod_row, final_norm, tile, final):
    t, d = x.shape
    return pl.pallas_call(
        functools.partial(_combine_kernel, tile=tile, final=final),
        grid=(t // tile,),
        in_specs=[
            pl.BlockSpec(memory_space=pl.ANY),
            pl.BlockSpec(memory_space=pl.ANY),
            pl.BlockSpec((tile, d), lambda i: (i, 0)),
            pl.BlockSpec((tile, TOP_K), lambda i: (i, 0)),
            pl.BlockSpec((1, 6, d), lambda i: (mod_row(i), 0, 0)),
            pl.BlockSpec((1, d), lambda i: (0, 0)),
        ],
        out_specs=pl.BlockSpec((tile, d), lambda i: (i, 0)),
        out_shape=jax.ShapeDtypeStruct((t, d), F32),
        scratch_shapes=[
            pltpu.SMEM((TOP_K * tile,), jnp.int32),
            pltpu.VMEM((TOP_K, tile, d), F32),
            pltpu.SemaphoreType.DMA,
            pltpu.SemaphoreType.DMA,
        ],
        compiler_params=_cparams("arbitrary"),
        name="moe_combine",
    )(dest_tiles, y, x, gate_t, mod, final_norm)


def _moe(h, x, mod, mod_row_of_token, w, final_norm, final):
    t, d = h.shape
    rows = MOE_ROWS
    tile = min(COPY_TILE, t)
    idx, gate, rank, counts = _route(h, w["w_router"], w["b_router"], ROUTER_TILE)
    padded = (counts + rows - 1) // rows * rows
    end_p = jnp.cumsum(padded)
    start_p = end_p - padded
    dest = start_p[idx] + rank
    n_rows = t * TOP_K + N_EXPERTS * rows
    n_blocks = n_rows // rows
    block_e = jnp.minimum(
        jnp.searchsorted(end_p, jnp.arange(n_blocks, dtype=jnp.int32) * rows, side="right"), N_EXPERTS - 1
    ).astype(jnp.int32)
    n_used = (end_p[-1] // rows).astype(jnp.int32).reshape(1)
    dest_tiles = dest.reshape(TOP_K, t // tile, tile).transpose(1, 0, 2).reshape(-1)
    xs = _dispatch(dest_tiles, h, n_rows, tile)
    y = _expert_ffn(block_e, n_used, xs, w, rows)
    return _combine(dest_tiles, y, x, gate.T, mod, lambda i: mod_row_of_token(i * tile), final_norm, tile, final)


def _log_sigmoid(x):
    return jnp.minimum(x, 0.0) - jnp.log1p(jnp.exp(-jnp.abs(x)))


def _gla_proj_kernel(x_ref, mod_ref, n1_ref, win_ref, wg1_ref, wg2_ref, bg_ref, q_ref, k_ref, v_ref, r_ref, g_ref,
                     *, nqk, nv, qscale):
    h = _rms(x_ref[...], n1_ref[...]) * (1.0 + _modrow(mod_ref, 1)) + _modrow(mod_ref, 0)
    hb = h.astype(BF16)
    a = jnp.dot(hb, win_ref[...], preferred_element_type=F32)
    q_ref[...] = a[:, :nqk] * qscale
    k_ref[...] = a[:, nqk:2 * nqk]
    v_ref[...] = a[:, 2 * nqk:2 * nqk + nv]
    r_ref[...] = a[:, 2 * nqk + nv:]
    lo = jnp.dot(hb, wg1_ref[...], preferred_element_type=F32)
    for dr in range(2):
        z = jnp.dot(lo[:, dr * GLA_GATE_RANK:(dr + 1) * GLA_GATE_RANK], wg2_ref[dr],
                    preferred_element_type=F32, precision=_HI) + bg_ref[dr]
        g_ref[dr] = _log_sigmoid(z) / GLA_GATE_NORM


def _gla_project(x, mod, mod_row_of_token, n1, w, tile):
    t, d = x.shape
    tile = min(tile, t)
    nqk = w["w_gate2"].shape[2]
    nv = (w["w_in"].shape[1] - 2 * nqk) // 2
    dk = nqk // GLA_HEADS
    const = lambda *shape: pl.BlockSpec(shape, lambda i: (0,) * len(shape))
    row = lambda n: pl.BlockSpec((tile, n), lambda i: (i, 0))
    return pl.pallas_call(
        functools.partial(_gla_proj_kernel, nqk=nqk, nv=nv, qscale=float(dk ** -0.5)),
        grid=(t // tile,),
        in_specs=[
            row(d),
            pl.BlockSpec((1, 6, d), lambda i: (mod_row_of_token(i * tile), 0, 0)),
            const(1, d),
            const(*w["w_in"].shape),
            const(*w["w_gate1"].shape),
            const(*w["w_gate2"].shape),
            const(*w["b_gate"].shape),
        ],
        out_specs=[row(nqk), row(nqk), row(nv), row(nv), pl.BlockSpec((2, tile, nqk), lambda i: (0, i, 0))],
        out_shape=[
            jax.ShapeDtypeStruct((t, nqk), F32),
            jax.ShapeDtypeStruct((t, nqk), F32),
            jax.ShapeDtypeStruct((t, nv), F32),
            jax.ShapeDtypeStruct((t, nv), F32),
            jax.ShapeDtypeStruct((2, t, nqk), F32),
        ],
        compiler_params=_cparams("parallel"),
        name="gla_project",
    )(x, mod, n1, w["w_in"], w["w_gate1"], w["w_gate2"], w["b_gate"])


def _gla_level_tables(chunk):
    n_lvl = int(np.log2(chunk))
    i = np.arange(chunk)[:, None]
    j = np.arange(chunk)[None, :]
    level = np.full((chunk, chunk), -1, np.int32)
    for lv in range(n_lvl):
        s = chunk >> (lv + 1)
        same = (i // (2 * s)) == (j // (2 * s))
        level[same & ((i % (2 * s)) >= s) & ((j % (2 * s)) < s)] = lv
    level[i == j] = n_lvl
    lower = (j <= i)
    cum = np.stack([lower, lower.T]).astype(np.float32)
    return np.stack([level, level.T]), np.concatenate([cum, cum], axis=2)


def _ref_rows(b, s):
    c, n = b.shape
    if s >= 4:
        return jnp.broadcast_to(b.reshape(c // (2 * s), 2 * s, n)[:, s:s + 1, :], (c // (2 * s), 2 * s, n)).reshape(c, n)
    pos = lax.broadcasted_iota(jnp.int32, (c, n), 0) % (2 * s)
    out = b
    for p in range(2 * s):
        if p != s:
            out = jnp.where(pos == p, pltpu.roll(b, (p - s) % c, axis=0), out)
    return out


def _gla_scan_kernel(q_ref, k_ref, v_ref, g_ref, lvl_ref, cum_ref, o_ref, st_sc, *, chunk, dk, dv, nch):
    dr = pl.program_id(1)

    @pl.when(pl.program_id(2) == 0)
    def _():
        st_sc[...] = jnp.zeros(st_sc.shape, F32)

    n_lvl = int(np.log2(chunk))
    lvl = lvl_ref[0]
    cum = cum_ref[0].astype(BF16)
    for ci in range(nch):
        c_eff = jnp.where(dr == 0, ci, nch - 1 - ci)
        rows = pl.ds(pl.multiple_of(c_eff * chunk, chunk), chunk)
        for hh in range(GLA_HEADS):
            qs = q_ref[rows, hh * dk:(hh + 1) * dk]
            ks = k_ref[rows, hh * dk:(hh + 1) * dk]
            vs = v_ref[rows, hh * dv:(hh + 1) * dv].astype(BF16)
            g = g_ref[0, rows, hh * dk:(hh + 1) * dk]
            g_hi = g.astype(BF16)
            g_lo = (g - g_hi.astype(F32)).astype(BF16)
            b = jnp.dot(cum, jnp.concatenate([g_hi, g_lo], axis=0), preferred_element_type=F32)
            b_tot = jnp.sum(g, axis=0, keepdims=True)
            st = st_sc[hh]
            q_in = (qs * jnp.exp(b)).astype(BF16)
            o = lax.dot_general(q_in, st.astype(BF16), _NT, preferred_element_type=F32)
            a = jnp.where(lvl == n_lvl, lax.dot_general(qs.astype(BF16), ks.astype(BF16), _NT, preferred_element_type=F32), 0.0)
            for lv in range(n_lvl):
                e = jnp.exp(-jnp.abs(b - _ref_rows(b, chunk >> (lv + 1))))
                al = lax.dot_general((qs * e).astype(BF16), (ks * e).astype(BF16), _NT, preferred_element_type=F32)
                a = jnp.where(lvl == lv, al, a)
            o = o + jnp.dot(a.astype(BF16), vs, preferred_element_type=F32)
            o_ref[0, rows, hh * dv:(hh + 1) * dv] = o
            k_out = (ks * jnp.exp(b_tot - b)).astype(BF16)
            st_sc[hh] = st * jnp.exp(b_tot) + lax.dot_general(vs, k_out, _TN, preferred_element_type=F32)


def _gla_scan(q, k, v, g, batch, n_lat, n_ctx):
    t, nqk = q.shape
    nv = v.shape[1]
    blk = GLA_BLOCK
    assert n_ctx == blk and n_lat % blk == 0
    nlat = n_lat // blk
    ctx0 = batch * nlat
    chunk = GLA_CHUNK
    lvl, cum = _gla_level_tables(chunk)

    def rowblk(bb, dr, j):
        lat = bb * nlat + jnp.where(dr == 0, j - 1, nlat - j)
        return jnp.where(j == 0, ctx0 + bb, lat)

    tok = lambda n: pl.BlockSpec((blk, n), lambda bb, dr, j: (rowblk(bb, dr, j), 0))
    return pl.pallas_call(
        functools.partial(_gla_scan_kernel, chunk=chunk, dk=nqk // GLA_HEADS, dv=nv // GLA_HEADS, nch=blk // chunk),
        grid=(batch, 2, nlat + 1),
        in_specs=[
            tok(nqk), tok(nqk), tok(nv),
            pl.BlockSpec((1, blk, nqk), lambda bb, dr, j: (dr, rowblk(bb, dr, j), 0)),
            pl.BlockSpec((1, chunk, chunk), lambda bb, dr, j: (dr, 0, 0)),
            pl.BlockSpec((1, chunk, 2 * chunk), lambda bb, dr, j: (dr, 0, 0)),
        ],
        out_specs=pl.BlockSpec((1, blk, nv), lambda bb, dr, j: (dr, rowblk(bb, dr, j), 0)),
        out_shape=jax.ShapeDtypeStruct((2, t, nv), F32),
        scratch_shapes=[pltpu.VMEM((GLA_HEADS, nv // GLA_HEADS, nqk // GLA_HEADS), F32)],
        compiler_params=_cparams("parallel", "parallel", "arbitrary"),
        name="gla_scan",
    )(q, k, v, g, jnp.asarray(lvl), jnp.asarray(cum))


def _gla_finish_kernel(o_ref, r_ref, x_ref, mod_ref, hn_ref, wo_ref, n2_ref, xo_ref, h2_ref, *, dv):
    o = o_ref[0] + o_ref[1]
    r = r_ref[...]
    parts = []
    for hh in range(GLA_HEADS):
        sl = slice(hh * dv, (hh + 1) * dv)
        parts.append(_rms(o[:, sl], hn_ref[...]) * (r[:, sl] * jax.nn.sigmoid(r[:, sl])))
    gated = jnp.concatenate(parts, axis=-1).astype(BF16)
    y = jnp.dot(gated, wo_ref[...], preferred_element_type=F32)
    xn = x_ref[...] + _modrow(mod_ref, 2) * y
    xo_ref[...] = xn
    h2_ref[...] = _rms(xn, n2_ref[...]) * (1.0 + _modrow(mod_ref, 4)) + _modrow(mod_ref, 3)


def _gla_finish(o, r, x, t, mod, mod_row_of_token, w, n2, tile):
    d = x.shape[1]
    nv = r.shape[1]
    tile = min(tile, t)
    const = lambda *shape: pl.BlockSpec(shape, lambda i: (0,) * len(shape))
    row = lambda n: pl.BlockSpec((tile, n), lambda i: (i, 0))
    return pl.pallas_call(
        functools.partial(_gla_finish_kernel, dv=nv // GLA_HEADS),
        grid=(t // tile,),
        in_specs=[
            pl.BlockSpec((2, tile, nv), lambda i: (0, i, 0)),
            row(nv), row(d),
            pl.BlockSpec((1, 6, d), lambda i: (mod_row_of_token(i * tile), 0, 0)),
            const(1, nv // GLA_HEADS),
            const(nv, d),
            const(1, d),
        ],
        out_specs=[row(d), row(d)],
        out_shape=[jax.ShapeDtypeStruct((t, d), F32), jax.ShapeDtypeStruct((t, d), F32)],
        compiler_params=_cparams("parallel"),
        name="gla_finish",
    )(o, r, x, mod, w["head_norm"], w["w_o"], n2)


def _rope_tables(n):
    rows = n // GRID_W
    row = jnp.repeat(jnp.arange(rows, dtype=F32), GRID_W)
    col = jnp.tile(jnp.arange(GRID_W, dtype=F32), rows)
    nf = MLA_ROPE // 4
    inv = jnp.power(ROPE_BASE, -jnp.arange(nf, dtype=F32) / nf)
    ang = jnp.concatenate([row[:, None] * inv, col[:, None] * inv], axis=-1)
    cos, sin = jnp.cos(ang), jnp.sin(ang)
    cr, cc, sr, sc = cos[:, :nf], cos[:, nf:], sin[:, :nf], sin[:, nf:]
    cos64 = jnp.concatenate([cr, cr, cc, cc], axis=-1)
    sin64 = jnp.concatenate([-sr, sr, -sc, sc], axis=-1)
    return jnp.tile(cos64, (1, MLA_HEADS)), jnp.tile(sin64, (1, MLA_HEADS))


def _swap_halves(w):
    nf = MLA_ROPE // 4
    perm = np.concatenate([np.arange(nf, 2 * nf), np.arange(0, nf), np.arange(3 * nf, 4 * nf), np.arange(2 * nf, 3 * nf)])
    return w[..., perm]


def _mla_weights(w_in, q_norm, w_uq, kv_norm, w_ukv, w_o):
    hh = MLA_HEADS
    o = MLA_Q_RANK + MLA_KV_RANK
    w_in_ext = jnp.concatenate([w_in, _swap_halves(w_in[:, o:])], axis=1)
    uq = w_uq.reshape(MLA_Q_RANK, hh, MLA_NOPE + MLA_ROPE)
    w_uk = w_ukv[:, :hh * MLA_NOPE].reshape(MLA_KV_RANK, hh, MLA_NOPE)
    w_uv = w_ukv[:, hh * MLA_NOPE:].reshape(MLA_KV_RANK, hh, MLA_V)
    return {
        "w_in": w_in_ext.astype(BF16),
        "q_norm": q_norm.reshape(1, -1),
        "kv_norm": kv_norm.reshape(1, -1),
        "w_qn": uq[:, :, :MLA_NOPE].reshape(MLA_Q_RANK, hh * MLA_NOPE).astype(BF16),
        "w_qr": uq[:, :, MLA_NOPE:].reshape(MLA_Q_RANK, hh * MLA_ROPE).astype(BF16),
        "w_qs": _swap_halves(uq[:, :, MLA_NOPE:]).reshape(MLA_Q_RANK, hh * MLA_ROPE).astype(BF16),
        "w_ukT": jnp.transpose(w_uk, (1, 2, 0)).astype(BF16),
        "w_uv": jnp.transpose(w_uv, (1, 0, 2)).astype(BF16),
        "w_o": w_o.astype(BF16),
    }


def _moe_weights(w_router, b_router, w_gu, b_gu, w_down, b_down):
    e = w_gu.shape[0]
    return {
        "w_router": w_router,
        "b_router": b_router,
        "w_g": w_gu[:, :, 0::2].astype(BF16),
        "w_u": w_gu[:, :, 1::2].astype(BF16),
        "b_g": b_gu[:, 0::2].reshape(e, 1, -1),
        "b_u": b_gu[:, 1::2].reshape(e, 1, -1),
        "w_d": w_down.astype(BF16),
        "b_d": b_down.reshape(e, 1, -1),
    }


def kernel(x, c, ctx, c_ctx, mod_w, mod_b, norm1, norm2, mla_w_in, mla_q_norm, mla_w_uq, mla_kv_norm, mla_w_ukv, mla_w_o, gla_w_in, gla_w_gate1, gla_w_gate2, gla_b_gate, gla_head_norm, gla_w_o, moe_w_router, moe_b_router, moe_w_gu, moe_b_gu, moe_w_down, moe_b_down, final_norm):
    B, N, D = x.shape
    Lc = ctx.shape[1]
    assert B < 8
    cc = jnp.zeros((8, D), F32).at[:B].set(c).at[B].set(c_ctx)
    mod = _modulation(cc, mod_w, mod_b).reshape(mod_w.shape[0], 8, 6, D)
    row_of_token = lambda tok: jnp.minimum(tok // N, B)
    n1 = norm1.reshape(-1, 1, D)
    n2 = norm2.reshape(-1, 1, D)
    fnorm = final_norm.reshape(1, D)

    mw = _mla_weights(mla_w_in[0], mla_q_norm[0], mla_w_uq[0], mla_kv_norm[0], mla_w_ukv[0], mla_w_o[0])
    cos, sin = _rope_tables(N)
    ones = jnp.ones((Lc, cos.shape[1]), F32)
    q_l, k_l = _mla_project(x, mod[0], lambda b: b, n1[0], mw, cos, sin, TOKEN_TILE)
    q_c, k_c = _mla_project(ctx, mod[0], lambda b: B, n1[0], mw, ones, jnp.zeros_like(ones), TOKEN_TILE)
    k_all = jnp.concatenate([k_l, k_c], axis=1)
    x1, h1 = _mla_attend(q_l, k_all, x, mod[0], lambda b: b, n2[0], mw, ATTN_Q_TILE, ATTN_KV_TILE)
    xc1, hc1 = _mla_attend(q_c, k_c, ctx, mod[0], lambda b: B, n2[0], mw, ATTN_Q_TILE, ATTN_KV_TILE)
    x_all = jnp.concatenate([x1.reshape(B * N, D), xc1.reshape(B * Lc, D)], axis=0)
    h_all = jnp.concatenate([h1.reshape(B * N, D), hc1.reshape(B * Lc, D)], axis=0)
    ew = _moe_weights(moe_w_router[0], moe_b_router[0], moe_w_gu[0], moe_b_gu[0], moe_w_down[0], moe_b_down[0])
    x_all = _moe(h_all, x_all, mod[0], row_of_token, ew, fnorm, final=False)

    gw = {
        "w_in": gla_w_in[0].astype(BF16),
        "w_gate1": jnp.concatenate([gla_w_gate1[0, 0], gla_w_gate1[0, 1]], axis=1).astype(BF16),
        "w_gate2": gla_w_gate2[0],
        "b_gate": gla_b_gate[0].reshape(2, 1, -1),
        "head_norm": gla_head_norm[0].reshape(1, -1),
        "w_o": gla_w_o[0].astype(BF16),
    }
    q, k, v, r, g = _gla_project(x_all, mod[1], row_of_token, n1[1], gw, TOKEN_TILE)
    o = _gla_scan(q, k, v, g, B, N, Lc)
    x2, h2 = _gla_finish(o, r, x_all, B * N, mod[1], row_of_token, gw, n2[1], TOKEN_TILE)
    ew = _moe_weights(moe_w_router[1], moe_b_router[1], moe_w_gu[1], moe_b_gu[1], moe_w_down[1], moe_b_down[1])
    out = _moe(h2, x2, mod[1], row_of_token, ew, fnorm, final=True)
    return out.reshape(B, N, D)
```

```python
import functools

import numpy as np
import jax
import jax.numpy as jnp
from jax import lax
from jax.experimental import pallas as pl
from jax.experimental.pallas import tpu as pltpu

F32 = jnp.float32
BF16 = jnp.bfloat16
EPS = 1e-6

GRID_W = 64
ROPE_BASE = 10000.0
MLA_HEADS = 8
MLA_Q_RANK = 512
MLA_KV_RANK = 256
MLA_NOPE = 128
MLA_ROPE = 64
MLA_V = 128
MLA_KDIM = MLA_KV_RANK + MLA_ROPE
GLA_HEADS = 4
GLA_GATE_RANK = 16
GLA_GATE_NORM = 16.0
GLA_CHUNK = 128
N_EXPERTS = 32
TOP_K = 4
SWIGLU_ALPHA = 1.702
SWIGLU_LIMIT = 7.0

VMEM_LIMIT_BYTES = 56 * 1024 * 1024
MOE_ROWS = 512
TOKEN_TILE = 512
ATTN_Q_TILE = 128
ATTN_KV_TILE = 768
ATTN_GROUPS = 1
ROUTER_TILE = 512
COPY_TILE = 256
GLA_BLOCK = 256

_HI = lax.Precision.HIGHEST
_NT = (((1,), (1,)), ((), ()))
_TN = (((0,), (0,)), ((), ()))


def _cparams(*sem):
    return pltpu.CompilerParams(dimension_semantics=sem, vmem_limit_bytes=VMEM_LIMIT_BYTES)


def _rms(x, g):
    return x * lax.rsqrt(jnp.mean(x * x, axis=-1, keepdims=True) + EPS) * g


def _modrow(mod_ref, k):
    return mod_ref[0, pl.ds(k, 1), :]


def _mod_kernel(cc_ref, w_ref, b_ref, o_ref):
    cc = cc_ref[...]
    s = cc * jax.nn.sigmoid(cc)
    o_ref[0] = jnp.dot(s, w_ref[0], preferred_element_type=F32, precision=_HI) + b_ref[0]


def _modulation(cc, mod_w, mod_b):
    depth, d, d6 = mod_w.shape
    tn = 1536
    return pl.pallas_call(
        _mod_kernel,
        grid=(depth, d6 // tn),
        in_specs=[
            pl.BlockSpec((8, d), lambda i, j: (0, 0)),
            pl.BlockSpec((1, d, tn), lambda i, j: (i, 0, j)),
            pl.BlockSpec((1, 1, tn), lambda i, j: (i, 0, j)),
        ],
        out_specs=pl.BlockSpec((1, 8, tn), lambda i, j: (i, 0, j)),
        out_shape=jax.ShapeDtypeStruct((depth, 8, d6), F32),
        compiler_params=_cparams("parallel", "parallel"),
        name="modulation",
    )(cc, mod_w, mod_b.reshape(depth, 1, d6))


def _mla_proj_kernel(x_ref, mod_ref, n1_ref, win_ref, qn_ref, kvn_ref, wqn_ref, wqr_ref, wqs_ref, wuk_ref,
                     cos_ref, sin_ref, q_ref, k_ref, *, scale):
    x = x_ref[0]
    h = _rms(x, n1_ref[...]) * (1.0 + _modrow(mod_ref, 1)) + _modrow(mod_ref, 0)
    a = jnp.dot(h.astype(BF16), win_ref[...], preferred_element_type=F32)
    cq = _rms(a[:, :MLA_Q_RANK], qn_ref[...]).astype(BF16)
    ckv = _rms(a[:, MLA_Q_RANK:MLA_Q_RANK + MLA_KV_RANK], kvn_ref[...])
    cos = cos_ref[...]
    sin = sin_ref[...]
    o = MLA_Q_RANK + MLA_KV_RANK
    kr = a[:, o:o + MLA_ROPE] * cos[:, :MLA_ROPE] + a[:, o + MLA_ROPE:o + 2 * MLA_ROPE] * sin[:, :MLA_ROPE]
    k_ref[0, :, :MLA_KV_RANK] = ckv.astype(BF16)
    k_ref[0, :, MLA_KV_RANK:] = kr.astype(BF16)
    qn = jnp.dot(cq, wqn_ref[...], preferred_element_type=F32)
    qr = (jnp.dot(cq, wqr_ref[...], preferred_element_type=F32) * cos
          + jnp.dot(cq, wqs_ref[...], preferred_element_type=F32) * sin)
    for hh in range(MLA_HEADS):
        qe = jnp.dot(qn[:, hh * MLA_NOPE:(hh + 1) * MLA_NOPE].astype(BF16), wuk_ref[hh], preferred_element_type=F32)
        q_ref[0, hh, :, :MLA_KV_RANK] = (qe * scale).astype(BF16)
        q_ref[0, hh, :, MLA_KV_RANK:] = (qr[:, hh * MLA_ROPE:(hh + 1) * MLA_ROPE] * scale).astype(BF16)


def _mla_project(x, mod, mod_row, n1, w, cos, sin, tile):
    b, l, d = x.shape
    tile = min(tile, l)
    assert l % tile == 0
    scale = float((MLA_NOPE + MLA_ROPE) ** -0.5 * np.log2(np.e))
    const = lambda *shape: pl.BlockSpec(shape, lambda i, j: (0,) * len(shape))
    return pl.pallas_call(
        functools.partial(_mla_proj_kernel, scale=scale),
        grid=(b, l // tile),
        in_specs=[
            pl.BlockSpec((1, tile, d), lambda i, j: (i, j, 0)),
            pl.BlockSpec((1, 6, d), lambda i, j: (mod_row(i), 0, 0)),
            const(1, d),
            const(*w["w_in"].shape),
            const(1, MLA_Q_RANK),
            const(1, MLA_KV_RANK),
            const(*w["w_qn"].shape),
            const(*w["w_qr"].shape),
            const(*w["w_qs"].shape),
            const(*w["w_ukT"].shape),
            pl.BlockSpec((tile, MLA_HEADS * MLA_ROPE), lambda i, j: (j, 0)),
            pl.BlockSpec((tile, MLA_HEADS * MLA_ROPE), lambda i, j: (j, 0)),
        ],
        out_specs=[
            pl.BlockSpec((1, MLA_HEADS, tile, MLA_KDIM), lambda i, j: (i, 0, j, 0)),
            pl.BlockSpec((1, tile, MLA_KDIM), lambda i, j: (i, j, 0)),
        ],
        out_shape=[
            jax.ShapeDtypeStruct((b, MLA_HEADS, l, MLA_KDIM), BF16),
            jax.ShapeDtypeStruct((b, l, MLA_KDIM), BF16),
        ],
        compiler_params=_cparams("parallel", "parallel"),
        name="mla_project",
    )(x, mod, n1, w["w_in"], w["q_norm"], w["kv_norm"], w["w_qn"], w["w_qr"], w["w_qs"], w["w_ukT"], cos, sin)


def _attn_kernel(q_ref, k_ref, x_ref, mod_ref, wuv_ref, wo_ref, n2_ref, xo_ref, h2_ref, *scratch, tq, tk, nkv):
    groups = len(scratch) // 5
    hg = MLA_HEADS // groups
    m_sc, l_sc, acc_sc = scratch[:groups], scratch[groups:2 * groups], scratch[2 * groups:3 * groups]
    s_sc = (scratch[3 * groups:4 * groups], scratch[4 * groups:])
    for g in range(groups):
        m_sc[g][...] = jnp.full(m_sc[g].shape, -jnp.inf, F32)
        l_sc[g][...] = jnp.zeros(l_sc[g].shape, F32)
        acc_sc[g][...] = jnp.zeros(acc_sc[g].shape, F32)

    def keys(c):
        return k_ref[0, pl.ds(pl.multiple_of(c * tk, tk), tk), :]

    def scores(c, slot):
        k = keys(c)
        for g in range(groups):
            q = q_ref[0, g * hg:(g + 1) * hg].reshape(hg * tq, MLA_KDIM)
            s_sc[slot][g][...] = lax.dot_general(q, k, _NT, preferred_element_type=F32)

    def absorb(c, slot):
        v = keys(c)[:, :MLA_KV_RANK]
        for g in range(groups):
            s = s_sc[slot][g][...]
            m_prev = m_sc[g][...]
            m_new = jnp.maximum(m_prev, jnp.max(s, axis=-1, keepdims=True))
            alpha = jnp.exp2(m_prev - m_new)
            p = jnp.exp2(s - m_new)
            l_sc[g][...] = alpha * l_sc[g][...] + jnp.sum(p, axis=-1, keepdims=True)
            acc_sc[g][...] = alpha * acc_sc[g][...] + jnp.dot(p.astype(BF16), v, preferred_element_type=F32)
            m_sc[g][...] = m_new

    def phase(c, slot, has_next):
        if has_next:
            scores(c + 1, 1 - slot)
        absorb(c, slot)

    scores(0, 0)
    npairs = (nkv - 1) // 2

    def body(j, carry):
        phase(2 * j, 0, True)
        phase(2 * j + 1, 1, True)
        return carry

    lax.fori_loop(0, npairs, body, 0)
    if nkv - 2 * npairs == 2:
        phase(nkv - 2, 0, True)
        phase(nkv - 1, 1, False)
    else:
        phase(nkv - 1, 0, False)
    u = []
    for hh in range(MLA_HEADS):
        g, r = divmod(hh, hg)
        o = (acc_sc[g][r * tq:(r + 1) * tq] / l_sc[g][r * tq:(r + 1) * tq]).astype(BF16)
        u.append(jnp.dot(o, wuv_ref[hh], preferred_element_type=F32))
    u = jnp.concatenate(u, axis=-1).astype(BF16)
    y = jnp.dot(u, wo_ref[...], preferred_element_type=F32)
    xn = x_ref[0] + _modrow(mod_ref, 2) * y
    xo_ref[0] = xn
    h2_ref[0] = _rms(xn, n2_ref[...]) * (1.0 + _modrow(mod_ref, 4)) + _modrow(mod_ref, 3)


def _mla_attend(q, keys, x, mod, mod_row, n2, w, tq, tk):
    b, l, d = x.shape
    lk = keys.shape[1]
    tq = min(tq, l)
    tk = min(tk, lk)
    assert lk % tk == 0 and l % tq == 0
    rows = MLA_HEADS // ATTN_GROUPS * tq
    const = lambda *shape: pl.BlockSpec(shape, lambda i, j: (0,) * len(shape))
    return pl.pallas_call(
        functools.partial(_attn_kernel, tq=tq, tk=tk, nkv=lk // tk),
        grid=(b, l // tq),
        in_specs=[
            pl.BlockSpec((1, MLA_HEADS, tq, MLA_KDIM), lambda i, j: (i, 0, j, 0)),
            pl.BlockSpec((1, lk, MLA_KDIM), lambda i, j: (i, 0, 0)),
            pl.BlockSpec((1, tq, d), lambda i, j: (i, j, 0)),
            pl.BlockSpec((1, 6, d), lambda i, j: (mod_row(i), 0, 0)),
            const(*w["w_uv"].shape),
            const(*w["w_o"].shape),
            const(1, d),
        ],
        out_specs=[
            pl.BlockSpec((1, tq, d), lambda i, j: (i, j, 0)),
            pl.BlockSpec((1, tq, d), lambda i, j: (i, j, 0)),
        ],
        out_shape=[jax.ShapeDtypeStruct((b, l, d), F32), jax.ShapeDtypeStruct((b, l, d), F32)],
        scratch_shapes=(
            [pltpu.VMEM((rows, 1), F32)] * (2 * ATTN_GROUPS)
            + [pltpu.VMEM((rows, MLA_KV_RANK), F32)] * ATTN_GROUPS
            + [pltpu.VMEM((rows, tk), F32)] * (2 * ATTN_GROUPS)
        ),
        compiler_params=_cparams("parallel", "arbitrary"),
        name="mla_attend",
    )(q, keys, x, mod, w["w_uv"], w["w_o"], n2)


def _router_kernel(h_ref, wt_ref, b_ref, idx_ref, gate_ref, rank_ref, cnt_ref, carry_sc, *, tile):
    @pl.when(pl.program_id(0) == 0)
    def _():
        carry_sc[...] = jnp.zeros(carry_sc.shape, F32)

    logits = lax.dot_general(wt_ref[...], h_ref[...], _NT, preferred_element_type=F32, precision=_HI) + b_ref[...]
    eid = lax.broadcasted_iota(jnp.int32, logits.shape, 0).astype(F32)
    vals, hots = [], []
    cur = logits
    for k in range(TOP_K):
        mx = jnp.max(cur, axis=0, keepdims=True)
        first = jnp.min(jnp.where(cur == mx, eid, float(N_EXPERTS)), axis=0, keepdims=True)
        hot = eid == first
        idx_ref[pl.ds(k, 1), :] = first.astype(jnp.int32)
        vals.append(mx)
        hots.append(hot)
        cur = jnp.where(hot, -jnp.inf, cur)
    ex = [jnp.exp(v - vals[0]) for v in vals]
    den = ex[0] + ex[1] + ex[2] + ex[3]
    for k in range(TOP_K):
        gate_ref[pl.ds(k, 1), :] = ex[k] / den
    cnt = sum(jnp.where(hot, 1.0, 0.0) for hot in hots)
    r = lax.broadcasted_iota(jnp.int32, (tile, tile), 0)
    c = lax.broadcasted_iota(jnp.int32, (tile, tile), 1)
    upper = jnp.where(r <= c, 1.0, 0.0).astype(BF16)
    incl = jnp.dot(cnt.astype(BF16), upper, preferred_element_type=F32)
    pos = carry_sc[...] + incl - 1.0
    for k in range(TOP_K):
        rank_ref[pl.ds(k, 1), :] = jnp.sum(jnp.where(hots[k], pos, 0.0), axis=0, keepdims=True).astype(jnp.int32)
    carry_sc[...] = carry_sc[...] + jnp.sum(cnt, axis=1, keepdims=True)
    cnt_ref[...] = jnp.broadcast_to(carry_sc[...], cnt_ref.shape)


def _route(h, w_router, b_router, tile):
    t, d = h.shape
    tile = min(tile, t)
    assert t % tile == 0
    tok = lambda i: (0, i)
    idx, gate, rank, cnt = pl.pallas_call(
        functools.partial(_router_kernel, tile=tile),
        grid=(t // tile,),
        in_specs=[
            pl.BlockSpec((tile, d), lambda i: (i, 0)),
            pl.BlockSpec((N_EXPERTS, d), lambda i: (0, 0)),
            pl.BlockSpec((N_EXPERTS, 1), lambda i: (0, 0)),
        ],
        out_specs=[
            pl.BlockSpec((TOP_K, tile), tok),
            pl.BlockSpec((TOP_K, tile), tok),
            pl.BlockSpec((TOP_K, tile), tok),
            pl.BlockSpec((N_EXPERTS, 128), lambda i: (0, 0)),
        ],
        out_shape=[
            jax.ShapeDtypeStruct((TOP_K, t), jnp.int32),
            jax.ShapeDtypeStruct((TOP_K, t), F32),
            jax.ShapeDtypeStruct((TOP_K, t), jnp.int32),
            jax.ShapeDtypeStruct((N_EXPERTS, 128), F32),
        ],
        scratch_shapes=[pltpu.VMEM((N_EXPERTS, 1), F32)],
        compiler_params=_cparams("arbitrary"),
        name="moe_router",
    )(h, w_router.T, b_router.reshape(N_EXPERTS, 1))
    return idx, gate, rank, cnt[:, 0].astype(jnp.int32)


def _row_copy(src_hbm, s, dst_ref, d, sem):
    return pltpu.make_async_copy(src_hbm.at[pl.ds(s, 1), :], dst_ref.at[pl.ds(d, 1), :], sem)


def _dispatch_kernel(dest_hbm, h_ref, xs_in, xs_hbm, dsm, isem, sem, *, tile):
    del xs_in
    i = pl.program_id(0)
    n = TOP_K * tile
    cp = pltpu.make_async_copy(dest_hbm.at[pl.ds(pl.multiple_of(i * n, n), n)], dsm, isem)
    cp.start()
    cp.wait()

    def issue(j, carry):
        for k in range(TOP_K):
            _row_copy(h_ref, j, xs_hbm, dsm[k * tile + j], sem).start()
        return carry

    lax.fori_loop(0, tile, issue, 0)
    for k in range(TOP_K):
        pltpu.make_async_copy(h_ref, xs_hbm.at[pl.ds(0, tile), :], sem).wait()


def _dispatch(dest_tiles, h, n_rows, tile):
    t, d = h.shape
    xs0 = jnp.zeros((n_rows, d), F32)
    return pl.pallas_call(
        functools.partial(_dispatch_kernel, tile=tile),
        grid=(t // tile,),
        in_specs=[
            pl.BlockSpec(memory_space=pl.ANY),
            pl.BlockSpec((tile, d), lambda i: (i, 0)),
            pl.BlockSpec(memory_space=pl.ANY),
        ],
        out_specs=pl.BlockSpec(memory_space=pl.ANY),
        out_shape=jax.ShapeDtypeStruct((n_rows, d), F32),
        scratch_shapes=[pltpu.SMEM((TOP_K * tile,), jnp.int32), pltpu.SemaphoreType.DMA, pltpu.SemaphoreType.DMA],
        input_output_aliases={2: 0},
        compiler_params=_cparams("arbitrary"),
        name="moe_dispatch",
    )(dest_tiles, h, xs0)


def _ffn_kernel(be_ref, nb_ref, xs_ref, wg_ref, wu_ref, bg_ref, bu_ref, wd_ref, bd_ref, y_ref):
    i = pl.program_id(0)

    @pl.when(i < nb_ref[0])
    def _():
        xb = xs_ref[...].astype(BF16)
        g = jnp.dot(xb, wg_ref[0], preferred_element_type=F32) + bg_ref[0]
        u = jnp.dot(xb, wu_ref[0], preferred_element_type=F32) + bu_ref[0]
        g = jnp.minimum(g, SWIGLU_LIMIT)
        u = jnp.clip(u, -SWIGLU_LIMIT, SWIGLU_LIMIT)
        a = (u + 1.0) * (g * jax.nn.sigmoid(SWIGLU_ALPHA * g))
        y_ref[...] = jnp.dot(a.astype(BF16), wd_ref[0], preferred_element_type=F32) + bd_ref[0]

    @pl.when(i >= nb_ref[0])
    def _():
        y_ref[...] = jnp.zeros(y_ref.shape, F32)


def _expert_ffn(block_e, n_used, xs, w, rows):
    n_rows, d = xs.shape
    f = w["w_g"].shape[2]
    ex = lambda i, be, nb: (be[i], 0, 0)
    return pl.pallas_call(
        _ffn_kernel,
        grid_spec=pltpu.PrefetchScalarGridSpec(
            num_scalar_prefetch=2,
            grid=(n_rows // rows,),
            in_specs=[
                pl.BlockSpec((rows, d), lambda i, be, nb: (i, 0)),
                pl.BlockSpec((1, d, f), ex),
                pl.BlockSpec((1, d, f), ex),
                pl.BlockSpec((1, 1, f), ex),
                pl.BlockSpec((1, 1, f), ex),
                pl.BlockSpec((1, f, d), ex),
                pl.BlockSpec((1, 1, d), ex),
            ],
            out_specs=pl.BlockSpec((rows, d), lambda i, be, nb: (i, 0)),
        ),
        out_shape=jax.ShapeDtypeStruct((n_rows, d), F32),
        compiler_params=_cparams("arbitrary"),
        name="moe_ffn",
    )(block_e, n_used, xs, w["w_g"], w["w_u"], w["b_g"], w["b_u"], w["w_d"], w["b_d"])


def _combine_kernel(dest_hbm, y_hbm, x_ref, gate_ref, mod_ref, fn_ref, o_ref, dsm, yg, isem, sem, *, tile, final):
    i = pl.program_id(0)
    n = TOP_K * tile
    cp = pltpu.make_async_copy(dest_hbm.at[pl.ds(pl.multiple_of(i * n, n), n)], dsm, isem)
    cp.start()
    cp.wait()

    def issue(j, carry):
        for k in range(TOP_K):
            _row_copy(y_hbm, dsm[k * tile + j], yg.at[k], j, sem).start()
        return carry

    lax.fori_loop(0, tile, issue, 0)
    for k in range(TOP_K):
        pltpu.make_async_copy(y_hbm.at[pl.ds(0, tile), :], yg.at[k], sem).wait()
    gate = gate_ref[...]
    acc = yg[0] * gate[:, 0:1]
    for k in range(1, TOP_K):
        acc = acc + yg[k] * gate[:, k:k + 1]
    xn = x_ref[...] + _modrow(mod_ref, 5) * acc
    o_ref[...] = _rms(xn, fn_ref[...]) if final else xn


def _combine(dest_tiles, y, x, gate_t, mod, mod_row, final_norm, tile, final):
    t, d = x.shape
    return pl.pallas_call(
        functools.partial(_combine_kernel, tile=tile, final=final),
        grid=(t // tile,),
        in_specs=[
            pl.BlockSpec(memory_space=pl.ANY),
            pl.BlockSpec(memory_space=pl.ANY),
            pl.BlockSpec((tile, d), lambda i: (i, 0)),
            pl.BlockSpec((tile, TOP_K), lambda i: (i, 0)),
            pl.BlockSpec((1, 6, d), lambda i: (mod_row(i), 0, 0)),
            pl.BlockSpec((1, d), lambda i: (0, 0)),
        ],
        out_specs=pl.BlockSpec((tile, d), lambda i: (i, 0)),
        out_shape=jax.ShapeDtypeStruct((t, d), F32),
        scratch_shapes=[
            pltpu.SMEM((TOP_K * tile,), jnp.int32),
            pltpu.VMEM((TOP_K, tile, d), F32),
            pltpu.SemaphoreType.DMA,
            pltpu.SemaphoreType.DMA,
        ],
        compiler_params=_cparams("arbitrary"),
        name="moe_combine",
    )(dest_tiles, y, x, gate_t, mod, final_norm)


def _moe(h, x, mod, mod_row_of_token, w, final_norm, final):
    t, d = h.shape
    rows = MOE_ROWS
    tile = min(COPY_TILE, t)
    assert t % tile == 0 and (TOP_K * tile) % 1024 == 0
    idx, gate, rank, counts = _route(h, w["w_router"], w["b_router"], ROUTER_TILE)
    padded = (counts + rows - 1) // rows * rows
    end_p = jnp.cumsum(padded)
    start_p = end_p - padded
    experts = jnp.arange(N_EXPERTS, dtype=jnp.int32)
    dest = jnp.sum(jnp.where(idx[:, :, None] == experts, start_p.astype(jnp.int32), 0), axis=-1) + rank
    n_rows = t * TOP_K + N_EXPERTS * rows
    n_blocks = n_rows // rows
    block_start = jnp.arange(n_blocks, dtype=jnp.int32) * rows
    block_e = jnp.minimum(jnp.sum(end_p[None, :] <= block_start[:, None], axis=1), N_EXPERTS - 1).astype(jnp.int32)
    n_used = (end_p[-1] // rows).astype(jnp.int32).reshape(1)
    dest_tiles = dest.reshape(TOP_K, t // tile, tile).transpose(1, 0, 2).reshape(-1)
    xs = _dispatch(dest_tiles, h, n_rows, tile)
    y = _expert_ffn(block_e, n_used, xs, w, rows)
    return _combine(dest_tiles, y, x, gate.T, mod, lambda i: mod_row_of_token(i * tile), final_norm, tile, final)


def _log_sigmoid(x):
    return jnp.minimum(x, 0.0) - jnp.log1p(jnp.exp(-jnp.abs(x)))


def _gla_proj_kernel(x_ref, mod_ref, n1_ref, win_ref, wg1_ref, wg2_ref, bg_ref, q_ref, k_ref, v_ref, r_ref, g_ref,
                     *, nqk, nv, qscale):
    h = _rms(x_ref[...], n1_ref[...]) * (1.0 + _modrow(mod_ref, 1)) + _modrow(mod_ref, 0)
    hb = h.astype(BF16)
    a = jnp.dot(hb, win_ref[...], preferred_element_type=F32)
    q_ref[...] = a[:, :nqk] * qscale
    k_ref[...] = a[:, nqk:2 * nqk]
    v_ref[...] = a[:, 2 * nqk:2 * nqk + nv]
    r_ref[...] = a[:, 2 * nqk + nv:]
    lo = jnp.dot(hb, wg1_ref[...], preferred_element_type=F32)
    for dr in range(2):
        z = jnp.dot(lo[:, dr * GLA_GATE_RANK:(dr + 1) * GLA_GATE_RANK], wg2_ref[dr],
                    preferred_element_type=F32, precision=_HI) + bg_ref[dr]
        g_ref[dr] = _log_sigmoid(z) / GLA_GATE_NORM


def _gla_project(x, mod, mod_row_of_token, n1, w, tile):
    t, d = x.shape
    tile = min(tile, t)
    assert t % tile == 0
    nqk = w["w_gate2"].shape[2]
    nv = (w["w_in"].shape[1] - 2 * nqk) // 2
    dk = nqk // GLA_HEADS
    const = lambda *shape: pl.BlockSpec(shape, lambda i: (0,) * len(shape))
    row = lambda n: pl.BlockSpec((tile, n), lambda i: (i, 0))
    return pl.pallas_call(
        functools.partial(_gla_proj_kernel, nqk=nqk, nv=nv, qscale=float(dk ** -0.5)),
        grid=(t // tile,),
        in_specs=[
            row(d),
            pl.BlockSpec((1, 6, d), lambda i: (mod_row_of_token(i * tile), 0, 0)),
            const(1, d),
            const(*w["w_in"].shape),
            const(*w["w_gate1"].shape),
            const(*w["w_gate2"].shape),
            const(*w["b_gate"].shape),
        ],
        out_specs=[row(nqk), row(nqk), row(nv), row(nv), pl.BlockSpec((2, tile, nqk), lambda i: (0, i, 0))],
        out_shape=[
            jax.ShapeDtypeStruct((t, nqk), F32),
            jax.ShapeDtypeStruct((t, nqk), F32),
            jax.ShapeDtypeStruct((t, nv), F32),
            jax.ShapeDtypeStruct((t, nv), F32),
            jax.ShapeDtypeStruct((2, t, nqk), F32),
        ],
        compiler_params=_cparams("parallel"),
        name="gla_project",
    )(x, mod, n1, w["w_in"], w["w_gate1"], w["w_gate2"], w["b_gate"])


def _gla_level_tables(chunk):
    n_lvl = int(np.log2(chunk))
    i = np.arange(chunk)[:, None]
    j = np.arange(chunk)[None, :]
    level = np.full((chunk, chunk), -1, np.int32)
    for lv in range(n_lvl):
        s = chunk >> (lv + 1)
        same = (i // (2 * s)) == (j // (2 * s))
        level[same & ((i % (2 * s)) >= s) & ((j % (2 * s)) < s)] = lv
    level[i == j] = n_lvl
    lower = (j <= i)
    cum = np.stack([lower, lower.T]).astype(np.float32)
    return np.stack([level, level.T]), np.concatenate([cum, cum], axis=2)


def _ref_rows(b, s):
    c, n = b.shape
    if s >= 4:
        return jnp.broadcast_to(b.reshape(c // (2 * s), 2 * s, n)[:, s:s + 1, :], (c // (2 * s), 2 * s, n)).reshape(c, n)
    pos = lax.broadcasted_iota(jnp.int32, (c, n), 0) % (2 * s)
    out = b
    for p in range(2 * s):
        if p != s:
            out = jnp.where(pos == p, pltpu.roll(b, (p - s) % c, axis=0), out)
    return out


def _gla_scan_kernel(q_ref, k_ref, v_ref, g_ref, lvl_ref, cum_ref, o_ref, st_sc, *, chunk, dk, dv, nch):
    dr = pl.program_id(1)

    @pl.when(pl.program_id(2) == 0)
    def _():
        st_sc[...] = jnp.zeros(st_sc.shape, F32)

    n_lvl = int(np.log2(chunk))
    lvl = lvl_ref[0]
    cum = cum_ref[0].astype(BF16)
    for ci in range(nch):
        c_eff = jnp.where(dr == 0, ci, nch - 1 - ci)
        rows = pl.ds(pl.multiple_of(c_eff * chunk, chunk), chunk)
        for hh in range(GLA_HEADS):
            qs = q_ref[rows, hh * dk:(hh + 1) * dk]
            ks = k_ref[rows, hh * dk:(hh + 1) * dk]
            vs = v_ref[rows, hh * dv:(hh + 1) * dv].astype(BF16)
            g = g_ref[0, rows, hh * dk:(hh + 1) * dk]
            g_hi = g.astype(BF16)
            g_lo = (g - g_hi.astype(F32)).astype(BF16)
            b = jnp.dot(cum, jnp.concatenate([g_hi, g_lo], axis=0), preferred_element_type=F32)
            b_tot = jnp.sum(g, axis=0, keepdims=True)
            st = st_sc[hh]
            q_in = (qs * jnp.exp(b)).astype(BF16)
            o = lax.dot_general(q_in, st.astype(BF16), _NT, preferred_element_type=F32)
            a = jnp.where(lvl == n_lvl, lax.dot_general(qs.astype(BF16), ks.astype(BF16), _NT, preferred_element_type=F32), 0.0)
            for lv in range(n_lvl):
                e = jnp.exp(-jnp.abs(b - _ref_rows(b, chunk >> (lv + 1))))
                al = lax.dot_general((qs * e).astype(BF16), (ks * e).astype(BF16), _NT, preferred_element_type=F32)
                a = jnp.where(lvl == lv, al, a)
            o = o + jnp.dot(a.astype(BF16), vs, preferred_element_type=F32)
            o_ref[0, rows, hh * dv:(hh + 1) * dv] = o
            k_out = (ks * jnp.exp(b_tot - b)).astype(BF16)
            st_sc[hh] = st * jnp.exp(b_tot) + lax.dot_general(vs, k_out, _TN, preferred_element_type=F32)


def _gla_scan(q, k, v, g, batch, n_lat, n_ctx):
    t, nqk = q.shape
    nv = v.shape[1]
    blk = GLA_BLOCK
    assert n_ctx == blk and n_lat % blk == 0
    nlat = n_lat // blk
    ctx0 = batch * nlat
    chunk = GLA_CHUNK
    lvl, cum = _gla_level_tables(chunk)

    def rowblk(bb, dr, j):
        lat = bb * nlat + jnp.where(dr == 0, j - 1, nlat - j)
        return jnp.where(j == 0, ctx0 + bb, lat)

    tok = lambda n: pl.BlockSpec((blk, n), lambda bb, dr, j: (rowblk(bb, dr, j), 0))
    return pl.pallas_call(
        functools.partial(_gla_scan_kernel, chunk=chunk, dk=nqk // GLA_HEADS, dv=nv // GLA_HEADS, nch=blk // chunk),
        grid=(batch, 2, nlat + 1),
        in_specs=[
            tok(nqk), tok(nqk), tok(nv),
            pl.BlockSpec((1, blk, nqk), lambda bb, dr, j: (dr, rowblk(bb, dr, j), 0)),
            pl.BlockSpec((1, chunk, chunk), lambda bb, dr, j: (dr, 0, 0)),
            pl.BlockSpec((1, chunk, 2 * chunk), lambda bb, dr, j: (dr, 0, 0)),
        ],
        out_specs=pl.BlockSpec((1, blk, nv), lambda bb, dr, j: (dr, rowblk(bb, dr, j), 0)),
        out_shape=jax.ShapeDtypeStruct((2, t, nv), F32),
        scratch_shapes=[pltpu.VMEM((GLA_HEADS, nv // GLA_HEADS, nqk // GLA_HEADS), F32)],
        compiler_params=_cparams("parallel", "parallel", "arbitrary"),
        name="gla_scan",
    )(q, k, v, g, jnp.asarray(lvl), jnp.asarray(cum))


def _gla_finish_kernel(o_ref, r_ref, x_ref, mod_ref, hn_ref, wo_ref, n2_ref, xo_ref, h2_ref, *, dv):
    o = o_ref[0] + o_ref[1]
    r = r_ref[...]
    parts = []
    for hh in range(GLA_HEADS):
        sl = slice(hh * dv, (hh + 1) * dv)
        parts.append(_rms(o[:, sl], hn_ref[...]) * (r[:, sl] * jax.nn.sigmoid(r[:, sl])))
    gated = jnp.concatenate(parts, axis=-1).astype(BF16)
    y = jnp.dot(gated, wo_ref[...], preferred_element_type=F32)
    xn = x_ref[...] + _modrow(mod_ref, 2) * y
    xo_ref[...] = xn
    h2_ref[...] = _rms(xn, n2_ref[...]) * (1.0 + _modrow(mod_ref, 4)) + _modrow(mod_ref, 3)


def _gla_finish(o, r, x, t, mod, mod_row_of_token, w, n2, tile):
    d = x.shape[1]
    nv = r.shape[1]
    tile = min(tile, t)
    assert t % tile == 0
    const = lambda *shape: pl.BlockSpec(shape, lambda i: (0,) * len(shape))
    row = lambda n: pl.BlockSpec((tile, n), lambda i: (i, 0))
    return pl.pallas_call(
        functools.partial(_gla_finish_kernel, dv=nv // GLA_HEADS),
        grid=(t // tile,),
        in_specs=[
            pl.BlockSpec((2, tile, nv), lambda i: (0, i, 0)),
            row(nv), row(d),
            pl.BlockSpec((1, 6, d), lambda i: (mod_row_of_token(i * tile), 0, 0)),
            const(1, nv // GLA_HEADS),
            const(nv, d),
            const(1, d),
        ],
        out_specs=[row(d), row(d)],
        out_shape=[jax.ShapeDtypeStruct((t, d), F32), jax.ShapeDtypeStruct((t, d), F32)],
        compiler_params=_cparams("parallel"),
        name="gla_finish",
    )(o, r, x, mod, w["head_norm"], w["w_o"], n2)


def _rope_tables(n):
    rows = n // GRID_W
    row = jnp.repeat(jnp.arange(rows, dtype=F32), GRID_W)
    col = jnp.tile(jnp.arange(GRID_W, dtype=F32), rows)
    nf = MLA_ROPE // 4
    inv = jnp.power(ROPE_BASE, -jnp.arange(nf, dtype=F32) / nf)
    ang = jnp.concatenate([row[:, None] * inv, col[:, None] * inv], axis=-1)
    cos, sin = jnp.cos(ang), jnp.sin(ang)
    cr, cc, sr, sc = cos[:, :nf], cos[:, nf:], sin[:, :nf], sin[:, nf:]
    cos64 = jnp.concatenate([cr, cr, cc, cc], axis=-1)
    sin64 = jnp.concatenate([-sr, sr, -sc, sc], axis=-1)
    return jnp.tile(cos64, (1, MLA_HEADS)), jnp.tile(sin64, (1, MLA_HEADS))


def _swap_halves(w):
    nf = MLA_ROPE // 4
    perm = np.concatenate([np.arange(nf, 2 * nf), np.arange(0, nf), np.arange(3 * nf, 4 * nf), np.arange(2 * nf, 3 * nf)])
    return w[..., perm]


def _mla_weights(w_in, q_norm, w_uq, kv_norm, w_ukv, w_o):
    hh = MLA_HEADS
    o = MLA_Q_RANK + MLA_KV_RANK
    w_in_ext = jnp.concatenate([w_in, _swap_halves(w_in[:, o:])], axis=1)
    uq = w_uq.reshape(MLA_Q_RANK, hh, MLA_NOPE + MLA_ROPE)
    w_uk = w_ukv[:, :hh * MLA_NOPE].reshape(MLA_KV_RANK, hh, MLA_NOPE)
    w_uv = w_ukv[:, hh * MLA_NOPE:].reshape(MLA_KV_RANK, hh, MLA_V)
    return {
        "w_in": w_in_ext.astype(BF16),
        "q_norm": q_norm.reshape(1, -1),
        "kv_norm": kv_norm.reshape(1, -1),
        "w_qn": uq[:, :, :MLA_NOPE].reshape(MLA_Q_RANK, hh * MLA_NOPE).astype(BF16),
        "w_qr": uq[:, :, MLA_NOPE:].reshape(MLA_Q_RANK, hh * MLA_ROPE).astype(BF16),
        "w_qs": _swap_halves(uq[:, :, MLA_NOPE:]).reshape(MLA_Q_RANK, hh * MLA_ROPE).astype(BF16),
        "w_ukT": jnp.transpose(w_uk, (1, 2, 0)).astype(BF16),
        "w_uv": jnp.transpose(w_uv, (1, 0, 2)).astype(BF16),
        "w_o": w_o.astype(BF16),
    }


def _split_gu_kernel(w_ref, g_ref, u_ref, *, cols):
    r = lax.broadcasted_iota(jnp.int32, (cols, cols), 0)
    c = lax.broadcasted_iota(jnp.int32, (cols, cols), 1)
    src = jnp.where(c < cols // 2, 2 * c, 2 * (c - cols // 2) + 1)
    perm = jnp.where(r == src, 1.0, 0.0).astype(BF16)
    out = jnp.dot(w_ref[0].astype(BF16), perm, preferred_element_type=F32).astype(BF16)
    g_ref[0] = out[:, :cols // 2]
    u_ref[0] = out[:, cols // 2:]


def _split_gate_up(w_gu):
    e, d, f2 = w_gu.shape
    cols = 512
    half = pl.BlockSpec((1, d, cols // 2), lambda i, j: (i, 0, j))
    return pl.pallas_call(
        functools.partial(_split_gu_kernel, cols=cols),
        grid=(e, f2 // cols),
        in_specs=[pl.BlockSpec((1, d, cols), lambda i, j: (i, 0, j))],
        out_specs=[half, half],
        out_shape=[jax.ShapeDtypeStruct((e, d, f2 // 2), BF16)] * 2,
        compiler_params=_cparams("parallel", "parallel"),
        name="moe_split_gate_up",
    )(w_gu)


def _moe_weights(w_router, b_router, w_gu, b_gu, w_down, b_down):
    e = w_gu.shape[0]
    w_g, w_u = _split_gate_up(w_gu)
    return {
        "w_router": w_router,
        "b_router": b_router,
        "w_g": w_g,
        "w_u": w_u,
        "b_g": b_gu[:, 0::2].reshape(e, 1, -1),
        "b_u": b_gu[:, 1::2].reshape(e, 1, -1),
        "w_d": w_down.astype(BF16),
        "b_d": b_down.reshape(e, 1, -1),
    }


def kernel(x, c, ctx, c_ctx, mod_w, mod_b, norm1, norm2, mla_w_in, mla_q_norm, mla_w_uq, mla_kv_norm, mla_w_ukv, mla_w_o, gla_w_in, gla_w_gate1, gla_w_gate2, gla_b_gate, gla_head_norm, gla_w_o, moe_w_router, moe_b_router, moe_w_gu, moe_b_gu, moe_w_down, moe_b_down, final_norm):
    B, N, D = x.shape
    Lc = ctx.shape[1]
    assert B < 8
    cc = jnp.zeros((8, D), F32).at[:B].set(c).at[B].set(c_ctx)
    mod = _modulation(cc, mod_w, mod_b).reshape(mod_w.shape[0], 8, 6, D)
    row_of_token = lambda tok: jnp.minimum(tok // N, B)
    n1 = norm1.reshape(-1, 1, D)
    n2 = norm2.reshape(-1, 1, D)
    fnorm = final_norm.reshape(1, D)

    mw = _mla_weights(mla_w_in[0], mla_q_norm[0], mla_w_uq[0], mla_kv_norm[0], mla_w_ukv[0], mla_w_o[0])
    cos, sin = _rope_tables(N)
    ones = jnp.ones((Lc, cos.shape[1]), F32)
    q_l, k_l = _mla_project(x, mod[0], lambda b: b, n1[0], mw, cos, sin, TOKEN_TILE)
    q_c, k_c = _mla_project(ctx, mod[0], lambda b: B, n1[0], mw, ones, jnp.zeros_like(ones), TOKEN_TILE)
    k_all = jnp.concatenate([k_l, k_c], axis=1)
    x1, h1 = _mla_attend(q_l, k_all, x, mod[0], lambda b: b, n2[0], mw, ATTN_Q_TILE, ATTN_KV_TILE)
    xc1, hc1 = _mla_attend(q_c, k_c, ctx, mod[0], lambda b: B, n2[0], mw, ATTN_Q_TILE, ATTN_KV_TILE)
    x_all = jnp.concatenate([x1.reshape(B * N, D), xc1.reshape(B * Lc, D)], axis=0)
    h_all = jnp.concatenate([h1.reshape(B * N, D), hc1.reshape(B * Lc, D)], axis=0)
    ew = _moe_weights(moe_w_router[0], moe_b_router[0], moe_w_gu[0], moe_b_gu[0], moe_w_down[0], moe_b_down[0])
    x_all = _moe(h_all, x_all, mod[0], row_of_token, ew, fnorm, final=False)

    gw = {
        "w_in": gla_w_in[0].astype(BF16),
        "w_gate1": jnp.concatenate([gla_w_gate1[0, 0], gla_w_gate1[0, 1]], axis=1).astype(BF16),
        "w_gate2": gla_w_gate2[0],
        "b_gate": gla_b_gate[0].reshape(2, 1, -1),
        "head_norm": gla_head_norm[0].reshape(1, -1),
        "w_o": gla_w_o[0].astype(BF16),
    }
    q, k, v, r, g = _gla_project(x_all, mod[1], row_of_token, n1[1], gw, TOKEN_TILE)
    o = _gla_scan(q, k, v, g, B, N, Lc)
    x2, h2 = _gla_finish(o, r, x_all, B * N, mod[1], row_of_token, gw, n2[1], TOKEN_TILE)
    ew = _moe_weights(moe_w_router[1], moe_b_router[1], moe_w_gu[1], moe_b_gu[1], moe_w_down[1], moe_b_down[1])
    out = _moe(h2, x2, mod[1], row_of_token, ew, fnorm, final=True)
    return out.reshape(B, N, D)
```

```python
import functools

import numpy as np
import jax
import jax.numpy as jnp
from jax import lax
from jax.experimental import pallas as pl
from jax.experimental.pallas import tpu as pltpu

F32 = jnp.float32
BF16 = jnp.bfloat16
EPS = 1e-6

GRID_W = 64
ROPE_BASE = 10000.0
MLA_HEADS = 8
MLA_Q_RANK = 512
MLA_KV_RANK = 256
MLA_NOPE = 128
MLA_ROPE = 64
MLA_V = 128
MLA_KDIM = MLA_KV_RANK + MLA_ROPE
GLA_HEADS = 4
GLA_GATE_RANK = 16
GLA_GATE_NORM = 16.0
GLA_CHUNK = 128
N_EXPERTS = 32
TOP_K = 4
SWIGLU_ALPHA = 1.702
SWIGLU_LIMIT = 7.0

VMEM_LIMIT_BYTES = 56 * 1024 * 1024
MOE_ROWS = 512
TOKEN_TILE = 512
ATTN_Q_TILE = 128
ATTN_KV_TILE = 768
ATTN_GROUPS = 1
ROUTER_TILE = 512
COPY_TILE = 256
GLA_BLOCK = 256

_HI = lax.Precision.HIGHEST
_NT = (((1,), (1,)), ((), ()))
_TN = (((0,), (0,)), ((), ()))


def _cparams(*sem):
    return pltpu.CompilerParams(dimension_semantics=sem, vmem_limit_bytes=VMEM_LIMIT_BYTES)


def _rms(x, g):
    return x * lax.rsqrt(jnp.mean(x * x, axis=-1, keepdims=True) + EPS) * g


def _modrow(mod_ref, k):
    return mod_ref[0, pl.ds(k, 1), :]


def _mod_kernel(cc_ref, w_ref, b_ref, o_ref):
    cc = cc_ref[...]
    s = cc * jax.nn.sigmoid(cc)
    o_ref[0] = jnp.dot(s, w_ref[0], preferred_element_type=F32, precision=_HI) + b_ref[0]


def _modulation(cc, mod_w, mod_b):
    depth, d, d6 = mod_w.shape
    tn = 1536
    return pl.pallas_call(
        _mod_kernel,
        grid=(depth, d6 // tn),
        in_specs=[
            pl.BlockSpec((8, d), lambda i, j: (0, 0)),
            pl.BlockSpec((1, d, tn), lambda i, j: (i, 0, j)),
            pl.BlockSpec((1, 1, tn), lambda i, j: (i, 0, j)),
        ],
        out_specs=pl.BlockSpec((1, 8, tn), lambda i, j: (i, 0, j)),
        out_shape=jax.ShapeDtypeStruct((depth, 8, d6), F32),
        compiler_params=_cparams("parallel", "parallel"),
        name="modulation",
    )(cc, mod_w, mod_b.reshape(depth, 1, d6))


def _mla_proj_kernel(x_ref, mod_ref, n1_ref, win_ref, qn_ref, kvn_ref, wqn_ref, wqr_ref, wqs_ref, wuk_ref,
                     cos_ref, sin_ref, q_ref, k_ref, *, scale):
    x = x_ref[0]
    h = _rms(x, n1_ref[...]) * (1.0 + _modrow(mod_ref, 1)) + _modrow(mod_ref, 0)
    a = jnp.dot(h.astype(BF16), win_ref[...], preferred_element_type=F32)
    cq = _rms(a[:, :MLA_Q_RANK], qn_ref[...]).astype(BF16)
    ckv = _rms(a[:, MLA_Q_RANK:MLA_Q_RANK + MLA_KV_RANK], kvn_ref[...])
    cos = cos_ref[...]
    sin = sin_ref[...]
    o = MLA_Q_RANK + MLA_KV_RANK
    kr = a[:, o:o + MLA_ROPE] * cos[:, :MLA_ROPE] + a[:, o + MLA_ROPE:o + 2 * MLA_ROPE] * sin[:, :MLA_ROPE]
    k_ref[0, :, :MLA_KV_RANK] = ckv.astype(BF16)
    k_ref[0, :, MLA_KV_RANK:] = kr.astype(BF16)
    qn = jnp.dot(cq, wqn_ref[...], preferred_element_type=F32)
    qr = (jnp.dot(cq, wqr_ref[...], preferred_element_type=F32) * cos
          + jnp.dot(cq, wqs_ref[...], preferred_element_type=F32) * sin)
    for hh in range(MLA_HEADS):
        qe = jnp.dot(qn[:, hh * MLA_NOPE:(hh + 1) * MLA_NOPE].astype(BF16), wuk_ref[hh], preferred_element_type=F32)
        q_ref[0, hh, :, :MLA_KV_RANK] = (qe * scale).astype(BF16)
        q_ref[0, hh, :, MLA_KV_RANK:] = (qr[:, hh * MLA_ROPE:(hh + 1) * MLA_ROPE] * scale).astype(BF16)


def _mla_project(x, mod, mod_row, n1, w, cos, sin, tile):
    b, l, d = x.shape
    tile = min(tile, l)
    assert l % tile == 0
    scale = float((MLA_NOPE + MLA_ROPE) ** -0.5 * np.log2(np.e))
    const = lambda *shape: pl.BlockSpec(shape, lambda i, j: (0,) * len(shape))
    return pl.pallas_call(
        functools.partial(_mla_proj_kernel, scale=scale),
        grid=(b, l // tile),
        in_specs=[
            pl.BlockSpec((1, tile, d), lambda i, j: (i, j, 0)),
            pl.BlockSpec((1, 6, d), lambda i, j: (mod_row(i), 0, 0)),
            const(1, d),
            const(*w["w_in"].shape),
            const(1, MLA_Q_RANK),
            const(1, MLA_KV_RANK),
            const(*w["w_qn"].shape),
            const(*w["w_qr"].shape),
            const(*w["w_qs"].shape),
            const(*w["w_ukT"].shape),
            pl.BlockSpec((tile, MLA_HEADS * MLA_ROPE), lambda i, j: (j, 0)),
            pl.BlockSpec((tile, MLA_HEADS * MLA_ROPE), lambda i, j: (j, 0)),
        ],
        out_specs=[
            pl.BlockSpec((1, MLA_HEADS, tile, MLA_KDIM), lambda i, j: (i, 0, j, 0)),
            pl.BlockSpec((1, tile, MLA_KDIM), lambda i, j: (i, j, 0)),
        ],
        out_shape=[
            jax.ShapeDtypeStruct((b, MLA_HEADS, l, MLA_KDIM), BF16),
            jax.ShapeDtypeStruct((b, l, MLA_KDIM), BF16),
        ],
        compiler_params=_cparams("parallel", "parallel"),
        name="mla_project",
    )(x, mod, n1, w["w_in"], w["q_norm"], w["kv_norm"], w["w_qn"], w["w_qr"], w["w_qs"], w["w_ukT"], cos, sin)


def _attn_kernel(q_ref, k_ref, x_ref, mod_ref, wuv_ref, wo_ref, n2_ref, xo_ref, h2_ref, *scratch, tq, tk, nkv):
    groups = len(scratch) // 5
    hg = MLA_HEADS // groups
    m_sc, l_sc, acc_sc = scratch[:groups], scratch[groups:2 * groups], scratch[2 * groups:3 * groups]
    s_sc = (scratch[3 * groups:4 * groups], scratch[4 * groups:])
    for g in range(groups):
        m_sc[g][...] = jnp.full(m_sc[g].shape, -jnp.inf, F32)
        l_sc[g][...] = jnp.zeros(l_sc[g].shape, F32)
        acc_sc[g][...] = jnp.zeros(acc_sc[g].shape, F32)

    def keys(c):
        return k_ref[0, pl.ds(pl.multiple_of(c * tk, tk), tk), :]

    def scores(c, slot):
        k = keys(c)
        for g in range(groups):
            q = q_ref[0, g * hg:(g + 1) * hg].reshape(hg * tq, MLA_KDIM)
            s_sc[slot][g][...] = lax.dot_general(q, k, _NT, preferred_element_type=F32)

    def absorb(c, slot):
        v = keys(c)[:, :MLA_KV_RANK]
        for g in range(groups):
            s = s_sc[slot][g][...]
            m_prev = m_sc[g][...]
            m_new = jnp.maximum(m_prev, jnp.max(s, axis=-1, keepdims=True))
            alpha = jnp.exp2(m_prev - m_new)
            p = jnp.exp2(s - m_new)
            l_sc[g][...] = alpha * l_sc[g][...] + jnp.sum(p, axis=-1, keepdims=True)
            acc_sc[g][...] = alpha * acc_sc[g][...] + jnp.dot(p.astype(BF16), v, preferred_element_type=F32)
            m_sc[g][...] = m_new

    def phase(c, slot, has_next):
        if has_next:
            scores(c + 1, 1 - slot)
        absorb(c, slot)

    scores(0, 0)
    npairs = (nkv - 1) // 2

    def body(j, carry):
        phase(2 * j, 0, True)
        phase(2 * j + 1, 1, True)
        return carry

    lax.fori_loop(0, npairs, body, 0)
    if nkv - 2 * npairs == 2:
        phase(nkv - 2, 0, True)
        phase(nkv - 1, 1, False)
    else:
        phase(nkv - 1, 0, False)
    u = []
    for hh in range(MLA_HEADS):
        g, r = divmod(hh, hg)
        o = (acc_sc[g][r * tq:(r + 1) * tq] / l_sc[g][r * tq:(r + 1) * tq]).astype(BF16)
        u.append(jnp.dot(o, wuv_ref[hh], preferred_element_type=F32))
    u = jnp.concatenate(u, axis=-1).astype(BF16)
    y = jnp.dot(u, wo_ref[...], preferred_element_type=F32)
    xn = x_ref[0] + _modrow(mod_ref, 2) * y
    xo_ref[...] = xn
    h2_ref[...] = _rms(xn, n2_ref[...]) * (1.0 + _modrow(mod_ref, 4)) + _modrow(mod_ref, 3)


def _mla_attend(q, keys, x, mod, mod_row, n2, w, tq, tk):
    b, l, d = x.shape
    lk = keys.shape[1]
    tq = min(tq, l)
    tk = min(tk, lk)
    assert lk % tk == 0 and l % tq == 0
    rows = MLA_HEADS // ATTN_GROUPS * tq
    const = lambda *shape: pl.BlockSpec(shape, lambda i, j: (0,) * len(shape))
    flat = pl.BlockSpec((tq, d), lambda i, j: (i * (l // tq) + j, 0))
    return pl.pallas_call(
        functools.partial(_attn_kernel, tq=tq, tk=tk, nkv=lk // tk),
        grid=(b, l // tq),
        in_specs=[
            pl.BlockSpec((1, MLA_HEADS, tq, MLA_KDIM), lambda i, j: (i, 0, j, 0)),
            pl.BlockSpec((1, lk, MLA_KDIM), lambda i, j: (i, 0, 0)),
            pl.BlockSpec((1, tq, d), lambda i, j: (i, j, 0)),
            pl.BlockSpec((1, 6, d), lambda i, j: (mod_row(i), 0, 0)),
            const(*w["w_uv"].shape),
            const(*w["w_o"].shape),
            const(1, d),
        ],
        out_specs=[flat, flat],
        out_shape=[jax.ShapeDtypeStruct((b * l, d), F32)] * 2,
        scratch_shapes=(
            [pltpu.VMEM((rows, 1), F32)] * (2 * ATTN_GROUPS)
            + [pltpu.VMEM((rows, MLA_KV_RANK), F32)] * ATTN_GROUPS
            + [pltpu.VMEM((rows, tk), F32)] * (2 * ATTN_GROUPS)
        ),
        compiler_params=_cparams("parallel", "arbitrary"),
        name="mla_attend",
    )(q, keys, x, mod, w["w_uv"], w["w_o"], n2)


def _router_kernel(h_ref, wt_ref, b_ref, idx_ref, gate_ref, rank_ref, cnt_ref, carry_sc, *, tile):
    @pl.when(pl.program_id(0) == 0)
    def _():
        carry_sc[...] = jnp.zeros(carry_sc.shape, F32)

    logits = lax.dot_general(wt_ref[...], h_ref[...], _NT, preferred_element_type=F32, precision=_HI) + b_ref[...]
    eid = lax.broadcasted_iota(jnp.int32, logits.shape, 0).astype(F32)
    vals, hots = [], []
    cur = logits
    for k in range(TOP_K):
        mx = jnp.max(cur, axis=0, keepdims=True)
        first = jnp.min(jnp.where(cur == mx, eid, float(N_EXPERTS)), axis=0, keepdims=True)
        hot = eid == first
        idx_ref[pl.ds(k, 1), :] = first.astype(jnp.int32)
        vals.append(mx)
        hots.append(hot)
        cur = jnp.where(hot, -jnp.inf, cur)
    ex = [jnp.exp(v - vals[0]) for v in vals]
    den = ex[0] + ex[1] + ex[2] + ex[3]
    for k in range(TOP_K):
        gate_ref[pl.ds(k, 1), :] = ex[k] / den
    cnt = sum(jnp.where(hot, 1.0, 0.0) for hot in hots)
    r = lax.broadcasted_iota(jnp.int32, (tile, tile), 0)
    c = lax.broadcasted_iota(jnp.int32, (tile, tile), 1)
    upper = jnp.where(r <= c, 1.0, 0.0).astype(BF16)
    incl = jnp.dot(cnt.astype(BF16), upper, preferred_element_type=F32)
    pos = carry_sc[...] + incl - 1.0
    for k in range(TOP_K):
        rank_ref[pl.ds(k, 1), :] = jnp.sum(jnp.where(hots[k], pos, 0.0), axis=0, keepdims=True).astype(jnp.int32)
    carry_sc[...] = carry_sc[...] + jnp.sum(cnt, axis=1, keepdims=True)
    cnt_ref[...] = jnp.broadcast_to(carry_sc[...], cnt_ref.shape)


def _route(h, w_router, b_router, tile):
    t, d = h.shape
    tile = min(tile, t)
    assert t % tile == 0
    tok = lambda i: (0, i)
    idx, gate, rank, cnt = pl.pallas_call(
        functools.partial(_router_kernel, tile=tile),
        grid=(t // tile,),
        in_specs=[
            pl.BlockSpec((tile, d), lambda i: (i, 0)),
            pl.BlockSpec((N_EXPERTS, d), lambda i: (0, 0)),
            pl.BlockSpec((N_EXPERTS, 1), lambda i: (0, 0)),
        ],
        out_specs=[
            pl.BlockSpec((TOP_K, tile), tok),
            pl.BlockSpec((TOP_K, tile), tok),
            pl.BlockSpec((TOP_K, tile), tok),
            pl.BlockSpec((N_EXPERTS, 128), lambda i: (0, 0)),
        ],
        out_shape=[
            jax.ShapeDtypeStruct((TOP_K, t), jnp.int32),
            jax.ShapeDtypeStruct((TOP_K, t), F32),
            jax.ShapeDtypeStruct((TOP_K, t), jnp.int32),
            jax.ShapeDtypeStruct((N_EXPERTS, 128), F32),
        ],
        scratch_shapes=[pltpu.VMEM((N_EXPERTS, 1), F32)],
        compiler_params=_cparams("arbitrary"),
        name="moe_router",
    )(h, w_router.T, b_router.reshape(N_EXPERTS, 1))
    return idx, gate, rank, cnt[:, 0].astype(jnp.int32)


SUBLANES = 8


def _dispatch_kernel(endp_ref, dest_hbm, h_ref, xs_hbm, dsm, zbuf, isem, sem, zsem, *, tile, rows):
    i = pl.program_id(0)
    n = TOP_K * tile

    @pl.when(i == 0)
    def _():
        zbuf[...] = jnp.zeros(zbuf.shape, F32)

        def span(e):
            start = pl.multiple_of(jnp.maximum(endp_ref[e] - rows, 0), rows)
            return pltpu.make_async_copy(zbuf, xs_hbm.at[pl.ds(start, rows), :], zsem)

        for e in range(N_EXPERTS):
            span(e).start()
        for e in range(N_EXPERTS):
            span(e).wait()
        n_rows = xs_hbm.shape[0]
        for e in range(N_EXPERTS):
            start = pl.multiple_of(endp_ref[N_EXPERTS - 1] + e * rows, rows)

            @pl.when(start < n_rows)
            def _():
                tail = pltpu.make_async_copy(zbuf, xs_hbm.at[pl.ds(start, rows), :], zsem)
                tail.start()
                tail.wait()

    cp = pltpu.make_async_copy(dest_hbm.at[pl.ds(pl.multiple_of(i * n, n), n)], dsm, isem)
    cp.start()
    cp.wait()

    def issue(grp, carry):
        for u in range(SUBLANES):
            src = h_ref.at[grp, pl.ds(u, 1), :]
            for k in range(TOP_K):
                dst = xs_hbm.at[pl.ds(dsm[k * tile + grp * SUBLANES + u], 1), :]
                pltpu.make_async_copy(src, dst, sem).start(priority=k % 2)
        return carry

    lax.fori_loop(0, tile // SUBLANES, issue, 0)
    for k in range(TOP_K):
        pltpu.make_async_copy(h_ref, h_ref, sem).wait()


def _dispatch(end_p, dest_tiles, h, n_rows, tile, rows):
    t, d = h.shape
    s = SUBLANES
    assert tile % s == 0 and rows % s == 0
    xs = pl.pallas_call(
        functools.partial(_dispatch_kernel, tile=tile, rows=rows),
        grid_spec=pltpu.PrefetchScalarGridSpec(
            num_scalar_prefetch=1,
            grid=(t // tile,),
            in_specs=[pl.BlockSpec(memory_space=pl.ANY), pl.BlockSpec((tile // s, s, d), lambda i, ep: (i, 0, 0))],
            out_specs=pl.BlockSpec(memory_space=pl.ANY),
            scratch_shapes=[
                pltpu.SMEM((TOP_K * tile,), jnp.int32),
                pltpu.VMEM((rows, d), F32),
                pltpu.SemaphoreType.DMA,
                pltpu.SemaphoreType.DMA,
                pltpu.SemaphoreType.DMA,
            ],
        ),
        out_shape=jax.ShapeDtypeStruct((n_rows, d), F32),
        compiler_params=_cparams("arbitrary"),
        name="moe_dispatch",
    )(end_p, dest_tiles, h.reshape(t // s, s, d))
    return xs


def _ffn_kernel(be_ref, nb_ref, xs_ref, wg_ref, wu_ref, bg_ref, bu_ref, wd_ref, bd_ref, y_ref):
    i = pl.program_id(0)

    @pl.when(i < nb_ref[0])
    def _():
        xb = xs_ref[...].astype(BF16)
        g = jnp.dot(xb, wg_ref[0], preferred_element_type=F32) + bg_ref[0]
        u = jnp.dot(xb, wu_ref[0], preferred_element_type=F32) + bu_ref[0]
        g = jnp.minimum(g, SWIGLU_LIMIT)
        u = jnp.clip(u, -SWIGLU_LIMIT, SWIGLU_LIMIT)
        a = (u + 1.0) * (g * jax.nn.sigmoid(SWIGLU_ALPHA * g))
        y_ref[...] = jnp.dot(a.astype(BF16), wd_ref[0, 0], preferred_element_type=F32) + bd_ref[0]

    @pl.when(i >= nb_ref[0])
    def _():
        y_ref[...] = jnp.zeros(y_ref.shape, F32)


def _expert_ffn(block_e, n_used, xs, w, rows):
    n_rows, d = xs.shape
    f = w["w_g"].shape[2]
    layer = w["layer"]
    ex = lambda i, be, nb: (be[i], 0, 0)
    return pl.pallas_call(
        _ffn_kernel,
        grid_spec=pltpu.PrefetchScalarGridSpec(
            num_scalar_prefetch=2,
            grid=(n_rows // rows,),
            in_specs=[
                pl.BlockSpec((rows, d), lambda i, be, nb: (jnp.minimum(i, nb[0] - 1), 0)),
                pl.BlockSpec((1, d, f), ex),
                pl.BlockSpec((1, d, f), ex),
                pl.BlockSpec((1, 1, f), ex),
                pl.BlockSpec((1, 1, f), ex),
                pl.BlockSpec((1, 1, f, d), lambda i, be, nb: (layer, be[i], 0, 0)),
                pl.BlockSpec((1, 1, d), ex),
            ],
            out_specs=pl.BlockSpec((rows, d), lambda i, be, nb: (i, 0)),
        ),
        out_shape=jax.ShapeDtypeStruct((n_rows, d), F32),
        compiler_params=_cparams("arbitrary"),
        name="moe_ffn",
    )(block_e, n_used, xs, w["w_g"], w["w_u"], w["b_g"], w["b_u"], w["w_d"], w["b_d"])


def _combine_kernel(dest_hbm, y_hbm, x_ref, gate_ref, mod_ref, fn_ref, o_ref, dsm, yg, isem, sem, *, tile, final):
    i = pl.program_id(0)
    n = TOP_K * tile
    cp = pltpu.make_async_copy(dest_hbm.at[pl.ds(pl.multiple_of(i * n, n), n)], dsm, isem)
    cp.start()
    cp.wait()

    def issue(grp, carry):
        for u in range(SUBLANES):
            for k in range(TOP_K):
                src = y_hbm.at[pl.ds(dsm[k * tile + grp * SUBLANES + u], 1), :]
                pltpu.make_async_copy(src, yg.at[k, grp, pl.ds(u, 1), :], sem).start(priority=k % 2)
        return carry

    lax.fori_loop(0, tile // SUBLANES, issue, 0)
    for k in range(TOP_K):
        pltpu.make_async_copy(yg.at[k], yg.at[k], sem).wait()
    gate = gate_ref[...]
    d = x_ref.shape[1]
    acc = yg[0].reshape(tile, d) * gate[:, 0:1]
    for k in range(1, TOP_K):
        acc = acc + yg[k].reshape(tile, d) * gate[:, k:k + 1]
    xn = x_ref[...] + _modrow(mod_ref, 5) * acc
    o_ref[...] = _rms(xn, fn_ref[...]) if final else xn


def _combine(dest_tiles, y, x, gate_t, mod, mod_row, final_norm, tile, final):
    t, d = x.shape
    s = SUBLANES
    return pl.pallas_call(
        functools.partial(_combine_kernel, tile=tile, final=final),
        grid=(t // tile,),
        in_specs=[
            pl.BlockSpec(memory_space=pl.ANY),
            pl.BlockSpec(memory_space=pl.ANY),
            pl.BlockSpec((tile, d), lambda i: (i, 0)),
            pl.BlockSpec((tile, TOP_K), lambda i: (i, 0)),
            pl.BlockSpec((1, 6, d), lambda i: (mod_row(i), 0, 0)),
            pl.BlockSpec((1, d), lambda i: (0, 0)),
        ],
        out_specs=pl.BlockSpec((tile, d), lambda i: (i, 0)),
        out_shape=jax.ShapeDtypeStruct((t, d), F32),
        scratch_shapes=[
            pltpu.SMEM((TOP_K * tile,), jnp.int32),
            pltpu.VMEM((TOP_K, tile // s, s, d), F32),
            pltpu.SemaphoreType.DMA,
            pltpu.SemaphoreType.DMA,
        ],
        compiler_params=_cparams("arbitrary"),
        name="moe_combine",
    )(dest_tiles, y, x, gate_t, mod, final_norm)


def _moe(h, x, mod, mod_row_of_token, w, final_norm, final):
    t, d = h.shape
    rows = MOE_ROWS
    tile = min(COPY_TILE, t)
    assert t % tile == 0 and (TOP_K * tile) % 1024 == 0
    idx, gate, rank, counts = _route(h, w["w_router"], w["b_router"], ROUTER_TILE)
    padded = (counts + rows - 1) // rows * rows
    end_p = jnp.cumsum(padded)
    start_p = end_p - padded
    experts = jnp.arange(N_EXPERTS, dtype=jnp.int32)
    dest = jnp.sum(jnp.where(idx[:, :, None] == experts, start_p.astype(jnp.int32), 0), axis=-1) + rank
    n_rows = t * TOP_K + N_EXPERTS * rows
    n_blocks = n_rows // rows
    block_start = jnp.arange(n_blocks, dtype=jnp.int32) * rows
    block_e = jnp.minimum(jnp.sum(end_p[None, :] <= block_start[:, None], axis=1), N_EXPERTS - 1).astype(jnp.int32)
    n_used = (end_p[-1] // rows).astype(jnp.int32).reshape(1)
    dest_tiles = dest.reshape(TOP_K, t // tile, tile).transpose(1, 0, 2).reshape(-1)
    xs = _dispatch(end_p.astype(jnp.int32), dest_tiles, h, n_rows, tile, rows)
    y = _expert_ffn(block_e, n_used, xs, w, rows)
    return _combine(dest_tiles, y, x, gate.T, mod, lambda i: mod_row_of_token(i * tile), final_norm, tile, final)


def _log_sigmoid(x):
    return jnp.minimum(x, 0.0) - jnp.log1p(jnp.exp(-jnp.abs(x)))


def _gla_proj_kernel(x_ref, mod_ref, n1_ref, win_ref, wg1_ref, wg2_ref, bg_ref, q_ref, k_ref, v_ref, r_ref, g_ref,
                     *, nqk, nv, qscale):
    h = _rms(x_ref[...], n1_ref[...]) * (1.0 + _modrow(mod_ref, 1)) + _modrow(mod_ref, 0)
    hb = h.astype(BF16)
    a = jnp.dot(hb, win_ref[...], preferred_element_type=F32)
    q_ref[...] = a[:, :nqk] * qscale
    k_ref[...] = a[:, nqk:2 * nqk]
    v_ref[...] = a[:, 2 * nqk:2 * nqk + nv]
    r_ref[...] = a[:, 2 * nqk + nv:]
    lo = jnp.dot(hb, wg1_ref[...], preferred_element_type=F32)
    for dr in range(2):
        z = jnp.dot(lo[:, dr * GLA_GATE_RANK:(dr + 1) * GLA_GATE_RANK], wg2_ref[dr],
                    preferred_element_type=F32, precision=_HI) + bg_ref[dr]
        g_ref[dr] = _log_sigmoid(z) * float(np.log2(np.e) / GLA_GATE_NORM)


def _gla_project(x, mod, mod_row_of_token, n1, w, tile):
    t, d = x.shape
    tile = min(tile, t)
    assert t % tile == 0
    nqk = w["w_gate2"].shape[2]
    nv = (w["w_in"].shape[1] - 2 * nqk) // 2
    dk = nqk // GLA_HEADS
    const = lambda *shape: pl.BlockSpec(shape, lambda i: (0,) * len(shape))
    row = lambda n: pl.BlockSpec((tile, n), lambda i: (i, 0))
    return pl.pallas_call(
        functools.partial(_gla_proj_kernel, nqk=nqk, nv=nv, qscale=float(dk ** -0.5)),
        grid=(t // tile,),
        in_specs=[
            row(d),
            pl.BlockSpec((1, 6, d), lambda i: (mod_row_of_token(i * tile), 0, 0)),
            const(1, d),
            const(*w["w_in"].shape),
            const(*w["w_gate1"].shape),
            const(*w["w_gate2"].shape),
            const(*w["b_gate"].shape),
        ],
        out_specs=[row(nqk), row(nqk), row(nv), row(nv), pl.BlockSpec((2, tile, nqk), lambda i: (0, i, 0))],
        out_shape=[
            jax.ShapeDtypeStruct((t, nqk), F32),
            jax.ShapeDtypeStruct((t, nqk), F32),
            jax.ShapeDtypeStruct((t, nv), F32),
            jax.ShapeDtypeStruct((t, nv), F32),
            jax.ShapeDtypeStruct((2, t, nqk), F32),
        ],
        compiler_params=_cparams("parallel"),
        name="gla_project",
    )(x, mod, n1, w["w_in"], w["w_gate1"], w["w_gate2"], w["b_gate"])


def _gla_level_tables(chunk):
    n_lvl = int(np.log2(chunk))
    i = np.arange(chunk)[:, None]
    j = np.arange(chunk)[None, :]
    level = np.full((chunk, chunk), -1, np.int32)
    for lv in range(n_lvl):
        s = chunk >> (lv + 1)
        same = (i // (2 * s)) == (j // (2 * s))
        level[same & ((i % (2 * s)) >= s) & ((j % (2 * s)) < s)] = lv
    level[i == j] = n_lvl
    lower = (j <= i)
    cum = np.stack([lower, lower.T]).astype(np.float32)
    return np.stack([level, level.T]), np.concatenate([cum, cum], axis=2)


def _ref_rows(b, s):
    c, n = b.shape
    if s >= 4:
        return jnp.broadcast_to(b.reshape(c // (2 * s), 2 * s, n)[:, s:s + 1, :], (c // (2 * s), 2 * s, n)).reshape(c, n)
    pos = lax.broadcasted_iota(jnp.int32, (c, n), 0) % (2 * s)
    out = b
    for p in range(2 * s):
        if p != s:
            out = jnp.where(pos == p, pltpu.roll(b, (p - s) % c, axis=0), out)
    return out


def _gla_scan_kernel(q_ref, k_ref, v_ref, g_ref, lvl_ref, cum_ref, o_ref, st_sc, *, chunk, dk, dv, nch):
    dr = pl.program_id(1)

    @pl.when(pl.program_id(2) == 0)
    def _():
        st_sc[...] = jnp.zeros(st_sc.shape, F32)

    n_lvl = int(np.log2(chunk))
    lvl = lvl_ref[0]
    cum = cum_ref[0].astype(BF16)
    for ci in range(nch):
        c_eff = jnp.where(dr == 0, ci, nch - 1 - ci)
        rows = pl.ds(pl.multiple_of(c_eff * chunk, chunk), chunk)
        for hh in range(GLA_HEADS):
            qs = q_ref[rows, hh * dk:(hh + 1) * dk]
            ks = k_ref[rows, hh * dk:(hh + 1) * dk]
            vs = v_ref[rows, hh * dv:(hh + 1) * dv].astype(BF16)
            g = g_ref[0, rows, hh * dk:(hh + 1) * dk]
            g_hi = g.astype(BF16)
            g_lo = (g - g_hi.astype(F32)).astype(BF16)
            b = jnp.dot(cum, jnp.concatenate([g_hi, g_lo], axis=0), preferred_element_type=F32)
            b_tot = jnp.sum(g, axis=0, keepdims=True)
            st = st_sc[hh]
            q_in = (qs * jnp.exp2(b)).astype(BF16)
            o = lax.dot_general(q_in, st.astype(BF16), _NT, preferred_element_type=F32)
            qb = qs.astype(BF16)
            kb = ks.astype(BF16)
            a = jnp.where(lvl == n_lvl, lax.dot_general(qb, kb, _NT, preferred_element_type=F32), 0.0)
            for lv in range(n_lvl):
                e = jnp.exp2(-jnp.abs(b - _ref_rows(b, chunk >> (lv + 1)))).astype(BF16)
                al = lax.dot_general(qb * e, kb * e, _NT, preferred_element_type=F32)
                a = jnp.where(lvl == lv, al, a)
            o = o + jnp.dot(a.astype(BF16), vs, preferred_element_type=F32)
            o_ref[0, rows, hh * dv:(hh + 1) * dv] = o
            k_out = (ks * jnp.exp2(b_tot - b)).astype(BF16)
            st_sc[hh] = st * jnp.exp2(b_tot) + lax.dot_general(vs, k_out, _TN, preferred_element_type=F32)


def _gla_scan(q, k, v, g, batch, n_lat, n_ctx):
    t, nqk = q.shape
    nv = v.shape[1]
    blk = GLA_BLOCK
    assert n_ctx == blk and n_lat % blk == 0
    nlat = n_lat // blk
    ctx0 = batch * nlat
    chunk = GLA_CHUNK
    lvl, cum = _gla_level_tables(chunk)

    def rowblk(bb, dr, j):
        lat = bb * nlat + jnp.where(dr == 0, j - 1, nlat - j)
        return jnp.where(j == 0, ctx0 + bb, lat)

    tok = lambda n: pl.BlockSpec((blk, n), lambda bb, dr, j: (rowblk(bb, dr, j), 0))
    return pl.pallas_call(
        functools.partial(_gla_scan_kernel, chunk=chunk, dk=nqk // GLA_HEADS, dv=nv // GLA_HEADS, nch=blk // chunk),
        grid=(batch, 2, nlat + 1),
        in_specs=[
            tok(nqk), tok(nqk), tok(nv),
            pl.BlockSpec((1, blk, nqk), lambda bb, dr, j: (dr, rowblk(bb, dr, j), 0)),
            pl.BlockSpec((1, chunk, chunk), lambda bb, dr, j: (dr, 0, 0)),
            pl.BlockSpec((1, chunk, 2 * chunk), lambda bb, dr, j: (dr, 0, 0)),
        ],
        out_specs=pl.BlockSpec((1, blk, nv), lambda bb, dr, j: (dr, rowblk(bb, dr, j), 0)),
        out_shape=jax.ShapeDtypeStruct((2, t, nv), F32),
        scratch_shapes=[pltpu.VMEM((GLA_HEADS, nv // GLA_HEADS, nqk // GLA_HEADS), F32)],
        compiler_params=_cparams("parallel", "parallel", "arbitrary"),
        name="gla_scan",
    )(q, k, v, g, jnp.asarray(lvl), jnp.asarray(cum))


def _gla_finish_kernel(o_ref, r_ref, x_ref, mod_ref, hn_ref, wo_ref, n2_ref, xo_ref, h2_ref, *, dv):
    o = o_ref[0] + o_ref[1]
    r = r_ref[...]
    parts = []
    for hh in range(GLA_HEADS):
        sl = slice(hh * dv, (hh + 1) * dv)
        parts.append(_rms(o[:, sl], hn_ref[...]) * (r[:, sl] * jax.nn.sigmoid(r[:, sl])))
    gated = jnp.concatenate(parts, axis=-1).astype(BF16)
    y = jnp.dot(gated, wo_ref[...], preferred_element_type=F32)
    xn = x_ref[...] + _modrow(mod_ref, 2) * y
    xo_ref[...] = xn
    h2_ref[...] = _rms(xn, n2_ref[...]) * (1.0 + _modrow(mod_ref, 4)) + _modrow(mod_ref, 3)


def _gla_finish(o, r, x, t, mod, mod_row_of_token, w, n2, tile):
    d = x.shape[1]
    nv = r.shape[1]
    tile = min(tile, t)
    assert t % tile == 0
    const = lambda *shape: pl.BlockSpec(shape, lambda i: (0,) * len(shape))
    row = lambda n: pl.BlockSpec((tile, n), lambda i: (i, 0))
    return pl.pallas_call(
        functools.partial(_gla_finish_kernel, dv=nv // GLA_HEADS),
        grid=(t // tile,),
        in_specs=[
            pl.BlockSpec((2, tile, nv), lambda i: (0, i, 0)),
            row(nv), row(d),
            pl.BlockSpec((1, 6, d), lambda i: (mod_row_of_token(i * tile), 0, 0)),
            const(1, nv // GLA_HEADS),
            const(nv, d),
            const(1, d),
        ],
        out_specs=[row(d), row(d)],
        out_shape=[jax.ShapeDtypeStruct((t, d), F32), jax.ShapeDtypeStruct((t, d), F32)],
        compiler_params=_cparams("parallel"),
        name="gla_finish",
    )(o, r, x, mod, w["head_norm"], w["w_o"], n2)


def _rope_tables(n):
    rows = n // GRID_W
    row = jnp.repeat(jnp.arange(rows, dtype=F32), GRID_W)
    col = jnp.tile(jnp.arange(GRID_W, dtype=F32), rows)
    nf = MLA_ROPE // 4
    inv = jnp.power(ROPE_BASE, -jnp.arange(nf, dtype=F32) / nf)
    ang = jnp.concatenate([row[:, None] * inv, col[:, None] * inv], axis=-1)
    cos, sin = jnp.cos(ang), jnp.sin(ang)
    cr, cc, sr, sc = cos[:, :nf], cos[:, nf:], sin[:, :nf], sin[:, nf:]
    cos64 = jnp.concatenate([cr, cr, cc, cc], axis=-1)
    sin64 = jnp.concatenate([-sr, sr, -sc, sc], axis=-1)
    return jnp.tile(cos64, (1, MLA_HEADS)), jnp.tile(sin64, (1, MLA_HEADS))


def _swap_halves(w):
    nf = MLA_ROPE // 4
    perm = np.concatenate([np.arange(nf, 2 * nf), np.arange(0, nf), np.arange(3 * nf, 4 * nf), np.arange(2 * nf, 3 * nf)])
    return w[..., perm]


def _mla_weights(w_in, q_norm, w_uq, kv_norm, w_ukv, w_o):
    hh = MLA_HEADS
    o = MLA_Q_RANK + MLA_KV_RANK
    w_in_ext = jnp.concatenate([w_in, _swap_halves(w_in[:, o:])], axis=1)
    uq = w_uq.reshape(MLA_Q_RANK, hh, MLA_NOPE + MLA_ROPE)
    w_uk = w_ukv[:, :hh * MLA_NOPE].reshape(MLA_KV_RANK, hh, MLA_NOPE)
    w_uv = w_ukv[:, hh * MLA_NOPE:].reshape(MLA_KV_RANK, hh, MLA_V)
    return {
        "w_in": w_in_ext.astype(BF16),
        "q_norm": q_norm.reshape(1, -1),
        "kv_norm": kv_norm.reshape(1, -1),
        "w_qn": uq[:, :, :MLA_NOPE].reshape(MLA_Q_RANK, hh * MLA_NOPE).astype(BF16),
        "w_qr": uq[:, :, MLA_NOPE:].reshape(MLA_Q_RANK, hh * MLA_ROPE).astype(BF16),
        "w_qs": _swap_halves(uq[:, :, MLA_NOPE:]).reshape(MLA_Q_RANK, hh * MLA_ROPE).astype(BF16),
        "w_ukT": jnp.transpose(w_uk, (1, 2, 0)).astype(BF16),
        "w_uv": jnp.transpose(w_uv, (1, 0, 2)).astype(BF16),
        "w_o": w_o.astype(BF16),
    }


def _split_gu_kernel(w_ref, g_ref, u_ref, *, cols):
    r = lax.broadcasted_iota(jnp.int32, (cols, cols), 0)
    c = lax.broadcasted_iota(jnp.int32, (cols, cols), 1)
    src = jnp.where(c < cols // 2, 2 * c, 2 * (c - cols // 2) + 1)
    perm = jnp.where(r == src, 1.0, 0.0).astype(BF16)
    out = jnp.dot(w_ref[0, 0].astype(BF16), perm, preferred_element_type=F32).astype(BF16)
    g_ref[0] = out[:, :cols // 2]
    u_ref[0] = out[:, cols // 2:]


def _split_gate_up(w_gu, layer):
    _, e, d, f2 = w_gu.shape
    cols = 512
    half = pl.BlockSpec((1, d, cols // 2), lambda i, j: (i, 0, j))
    return pl.pallas_call(
        functools.partial(_split_gu_kernel, cols=cols),
        grid=(e, f2 // cols),
        in_specs=[pl.BlockSpec((1, 1, d, cols), lambda i, j: (layer, i, 0, j))],
        out_specs=[half, half],
        out_shape=[jax.ShapeDtypeStruct((e, d, f2 // 2), BF16)] * 2,
        compiler_params=_cparams("parallel", "parallel"),
        name="moe_split_gate_up",
    )(w_gu)


def _moe_weights(layer, w_router, b_router, w_gu, b_gu, w_down_bf16, b_down):
    e = w_gu.shape[1]
    w_g, w_u = _split_gate_up(w_gu, layer)
    return {
        "layer": layer,
        "w_router": w_router[layer],
        "b_router": b_router[layer],
        "w_g": w_g,
        "w_u": w_u,
        "b_g": b_gu[layer, :, 0::2].reshape(e, 1, -1),
        "b_u": b_gu[layer, :, 1::2].reshape(e, 1, -1),
        "w_d": w_down_bf16,
        "b_d": b_down[layer].reshape(e, 1, -1),
    }


def kernel(x, c, ctx, c_ctx, mod_w, mod_b, norm1, norm2, mla_w_in, mla_q_norm, mla_w_uq, mla_kv_norm, mla_w_ukv, mla_w_o, gla_w_in, gla_w_gate1, gla_w_gate2, gla_b_gate, gla_head_norm, gla_w_o, moe_w_router, moe_b_router, moe_w_gu, moe_b_gu, moe_w_down, moe_b_down, final_norm):
    B, N, D = x.shape
    Lc = ctx.shape[1]
    assert B < 8
    cc = jnp.zeros((8, D), F32).at[:B].set(c).at[B].set(c_ctx)
    mod = _modulation(cc, mod_w, mod_b).reshape(mod_w.shape[0], 8, 6, D)
    row_of_token = lambda tok: jnp.minimum(tok // N, B)
    n1 = norm1.reshape(-1, 1, D)
    n2 = norm2.reshape(-1, 1, D)
    fnorm = final_norm.reshape(1, D)

    mw = _mla_weights(mla_w_in[0], mla_q_norm[0], mla_w_uq[0], mla_kv_norm[0], mla_w_ukv[0], mla_w_o[0])
    cos, sin = _rope_tables(N)
    ones = jnp.ones((Lc, cos.shape[1]), F32)
    q_l, k_l = _mla_project(x, mod[0], lambda b: b, n1[0], mw, cos, sin, TOKEN_TILE)
    q_c, k_c = _mla_project(ctx, mod[0], lambda b: B, n1[0], mw, ones, jnp.zeros_like(ones), TOKEN_TILE)
    k_all = jnp.concatenate([k_l, k_c], axis=1)
    x1, h1 = _mla_attend(q_l, k_all, x, mod[0], lambda b: b, n2[0], mw, ATTN_Q_TILE, ATTN_KV_TILE)
    xc1, hc1 = _mla_attend(q_c, k_c, ctx, mod[0], lambda b: B, n2[0], mw, ATTN_Q_TILE, ATTN_KV_TILE)
    x_all = jnp.concatenate([x1, xc1], axis=0)
    h_all = jnp.concatenate([h1, hc1], axis=0)
    w_down_bf16 = moe_w_down.astype(BF16)
    ew = _moe_weights(0, moe_w_router, moe_b_router, moe_w_gu, moe_b_gu, w_down_bf16, moe_b_down)
    x_all = _moe(h_all, x_all, mod[0], row_of_token, ew, fnorm, final=False)

    gw = {
        "w_in": gla_w_in[0].astype(BF16),
        "w_gate1": jnp.concatenate([gla_w_gate1[0, 0], gla_w_gate1[0, 1]], axis=1).astype(BF16),
        "w_gate2": gla_w_gate2[0],
        "b_gate": gla_b_gate[0].reshape(2, 1, -1),
        "head_norm": gla_head_norm[0].reshape(1, -1),
        "w_o": gla_w_o[0].astype(BF16),
    }
    q, k, v, r, g = _gla_project(x_all, mod[1], row_of_token, n1[1], gw, TOKEN_TILE)
    o = _gla_scan(q, k, v, g, B, N, Lc)
    x2, h2 = _gla_finish(o, r, x_all, B * N, mod[1], row_of_token, gw, n2[1], TOKEN_TILE)
    ew = _moe_weights(1, moe_w_router, moe_b_router, moe_w_gu, moe_b_gu, w_down_bf16, moe_b_down)
    out = _moe(h2, x2, mod[1], row_of_token, ew, fnorm, final=True)
    return out.reshape(B, N, D)
```

```python
import functools

import numpy as np
import jax
import jax.numpy as jnp
from jax import lax
from jax.experimental import pallas as pl
from jax.experimental.pallas import tpu as pltpu

F32 = jnp.float32
BF16 = jnp.bfloat16
EPS = 1e-6

GRID_W = 64
ROPE_BASE = 10000.0
MLA_HEADS = 8
MLA_Q_RANK = 512
MLA_KV_RANK = 256
MLA_NOPE = 128
MLA_ROPE = 64
MLA_V = 128
MLA_KDIM = MLA_KV_RANK + MLA_ROPE
GLA_HEADS = 4
GLA_GATE_RANK = 16
GLA_GATE_NORM = 16.0
GLA_CHUNK = 128
N_EXPERTS = 32
TOP_K = 4
SWIGLU_ALPHA = 1.702
SWIGLU_LIMIT = 7.0

VMEM_LIMIT_BYTES = 56 * 1024 * 1024
MOE_ROWS = 512
TOKEN_TILE = 512
ATTN_Q_TILE = 128
ATTN_KV_TILE = 768
ROUTER_TILE = 512
COPY_TILE = 512
GLA_BLOCK = 256

_HI = lax.Precision.HIGHEST
_NT = (((1,), (1,)), ((), ()))
_TN = (((0,), (0,)), ((), ()))


def _cparams(*sem):
    return pltpu.CompilerParams(dimension_semantics=sem, vmem_limit_bytes=VMEM_LIMIT_BYTES)


def _rms(x, g):
    return x * lax.rsqrt(jnp.mean(x * x, axis=-1, keepdims=True) + EPS) * g


def _modrow(mod_ref, k):
    return mod_ref[0, pl.ds(k, 1), :]


def _mod_kernel(cc_ref, w_ref, b_ref, o_ref):
    cc = cc_ref[...]
    s = cc * jax.nn.sigmoid(cc)
    o_ref[0] = jnp.dot(s, w_ref[0], preferred_element_type=F32, precision=_HI) + b_ref[0]


def _modulation(cc, mod_w, mod_b):
    depth, d, d6 = mod_w.shape
    tn = 1536
    return pl.pallas_call(
        _mod_kernel,
        grid=(depth, d6 // tn),
        in_specs=[
            pl.BlockSpec((8, d), lambda i, j: (0, 0)),
            pl.BlockSpec((1, d, tn), lambda i, j: (i, 0, j)),
            pl.BlockSpec((1, 1, tn), lambda i, j: (i, 0, j)),
        ],
        out_specs=pl.BlockSpec((1, 8, tn), lambda i, j: (i, 0, j)),
        out_shape=jax.ShapeDtypeStruct((depth, 8, d6), F32),
        compiler_params=_cparams("parallel", "parallel"),
        name="modulation",
    )(cc, mod_w, mod_b.reshape(depth, 1, d6))


def _mla_proj_kernel(x_ref, mod_ref, n1_ref, win_ref, qn_ref, kvn_ref, wqn_ref, wqr_ref, wqs_ref, wuk_ref,
                     cos_ref, sin_ref, q_ref, k_ref, *, scale):
    x = x_ref[0]
    h = _rms(x, n1_ref[...]) * (1.0 + _modrow(mod_ref, 1)) + _modrow(mod_ref, 0)
    a = jnp.dot(h.astype(BF16), win_ref[...], preferred_element_type=F32)
    cq = _rms(a[:, :MLA_Q_RANK], qn_ref[...]).astype(BF16)
    ckv = _rms(a[:, MLA_Q_RANK:MLA_Q_RANK + MLA_KV_RANK], kvn_ref[...])
    cos = cos_ref[...]
    sin = sin_ref[...]
    o = MLA_Q_RANK + MLA_KV_RANK
    kr = a[:, o:o + MLA_ROPE] * cos[:, :MLA_ROPE] + a[:, o + MLA_ROPE:o + 2 * MLA_ROPE] * sin[:, :MLA_ROPE]
    k_ref[0, :, :MLA_KV_RANK] = ckv.astype(BF16)
    k_ref[0, :, MLA_KV_RANK:] = kr.astype(BF16)
    qn = jnp.dot(cq, wqn_ref[...], preferred_element_type=F32)
    qr = (jnp.dot(cq, wqr_ref[...], preferred_element_type=F32) * cos
          + jnp.dot(cq, wqs_ref[...], preferred_element_type=F32) * sin)
    for hh in range(MLA_HEADS):
        qe = jnp.dot(qn[:, hh * MLA_NOPE:(hh + 1) * MLA_NOPE].astype(BF16), wuk_ref[hh], preferred_element_type=F32)
        q_ref[0, hh, :, :MLA_KV_RANK] = (qe * scale).astype(BF16)
        q_ref[0, hh, :, MLA_KV_RANK:] = (qr[:, hh * MLA_ROPE:(hh + 1) * MLA_ROPE] * scale).astype(BF16)


def _mla_project(x, mod, mod_row, n1, w, cos, sin, tile):
    b, l, d = x.shape
    tile = min(tile, l)
    assert l % tile == 0
    scale = float((MLA_NOPE + MLA_ROPE) ** -0.5 * np.log2(np.e))
    const = lambda *shape: pl.BlockSpec(shape, lambda i, j: (0,) * len(shape))
    return pl.pallas_call(
        functools.partial(_mla_proj_kernel, scale=scale),
        grid=(b, l // tile),
        in_specs=[
            pl.BlockSpec((1, tile, d), lambda i, j: (i, j, 0)),
            pl.BlockSpec((1, 6, d), lambda i, j: (mod_row(i), 0, 0)),
            const(1, d),
            const(*w["w_in"].shape),
            const(1, MLA_Q_RANK),
            const(1, MLA_KV_RANK),
            const(*w["w_qn"].shape),
            const(*w["w_qr"].shape),
            const(*w["w_qs"].shape),
            const(*w["w_ukT"].shape),
            pl.BlockSpec((tile, MLA_HEADS * MLA_ROPE), lambda i, j: (j, 0)),
            pl.BlockSpec((tile, MLA_HEADS * MLA_ROPE), lambda i, j: (j, 0)),
        ],
        out_specs=[
            pl.BlockSpec((1, MLA_HEADS, tile, MLA_KDIM), lambda i, j: (i, 0, j, 0)),
            pl.BlockSpec((1, tile, MLA_KDIM), lambda i, j: (i, j, 0)),
        ],
        out_shape=[
            jax.ShapeDtypeStruct((b, MLA_HEADS, l, MLA_KDIM), BF16),
            jax.ShapeDtypeStruct((b, l, MLA_KDIM), BF16),
        ],
        compiler_params=_cparams("parallel", "parallel"),
        name="mla_project",
    )(x, mod, n1, w["w_in"], w["q_norm"], w["kv_norm"], w["w_qn"], w["w_qr"], w["w_qs"], w["w_ukT"], cos, sin)


def _attn_kernel(q_ref, k_ref, x_ref, mod_ref, wuv_ref, wo_ref, n2_ref, xo_ref, h2_ref, m_sc, l_sc, acc_sc, *s_sc,
                 tq, tk, nkv):
    rows = MLA_HEADS * tq
    m_sc[...] = jnp.full(m_sc.shape, -jnp.inf, F32)
    l_sc[...] = jnp.zeros(l_sc.shape, F32)
    acc_sc[...] = jnp.zeros(acc_sc.shape, F32)

    def keys(c):
        return k_ref[0, pl.ds(pl.multiple_of(c * tk, tk), tk), :]

    def scores(c, slot):
        s_sc[slot][...] = lax.dot_general(q_ref[0].reshape(rows, MLA_KDIM), keys(c), _NT, preferred_element_type=F32)

    def absorb(c, slot):
        v = keys(c)[:, :MLA_KV_RANK]
        s = s_sc[slot][...]
        m_prev = m_sc[...]
        m_new = jnp.maximum(m_prev, jnp.max(s, axis=-1, keepdims=True))
        alpha = jnp.exp2(m_prev - m_new)
        p = jnp.exp2(s - m_new)
        l_sc[...] = alpha * l_sc[...] + jnp.sum(p, axis=-1, keepdims=True)
        acc_sc[...] = alpha * acc_sc[...] + jnp.dot(p.astype(BF16), v, preferred_element_type=F32)
        m_sc[...] = m_new

    def phase(c, slot, has_next):
        if has_next:
            scores(c + 1, 1 - slot)
        absorb(c, slot)

    scores(0, 0)
    npairs = (nkv - 1) // 2

    def body(j, carry):
        phase(2 * j, 0, True)
        phase(2 * j + 1, 1, True)
        return carry

    lax.fori_loop(0, npairs, body, 0)
    if nkv - 2 * npairs == 2:
        phase(nkv - 2, 0, True)
        phase(nkv - 1, 1, False)
    else:
        phase(nkv - 1, 0, False)
    u = []
    for hh in range(MLA_HEADS):
        o = (acc_sc[hh * tq:(hh + 1) * tq] / l_sc[hh * tq:(hh + 1) * tq]).astype(BF16)
        u.append(jnp.dot(o, wuv_ref[hh], preferred_element_type=F32))
    u = jnp.concatenate(u, axis=-1).astype(BF16)
    y = jnp.dot(u, wo_ref[...], preferred_element_type=F32)
    xn = x_ref[0] + _modrow(mod_ref, 2) * y
    xo_ref[...] = xn
    h2_ref[...] = _rms(xn, n2_ref[...]) * (1.0 + _modrow(mod_ref, 4)) + _modrow(mod_ref, 3)


def _mla_attend(q, keys, x, mod, mod_row, n2, w, tq, tk):
    b, l, d = x.shape
    lk = keys.shape[1]
    tq = min(tq, l)
    tk = min(tk, lk)
    assert lk % tk == 0 and l % tq == 0
    rows = MLA_HEADS * tq
    nq = l // tq
    const = lambda *shape: pl.BlockSpec(shape, lambda i, j: (0,) * len(shape))
    flat = pl.BlockSpec((tq, d), lambda i, j: (i * nq + j, 0))
    return pl.pallas_call(
        functools.partial(_attn_kernel, tq=tq, tk=tk, nkv=lk // tk),
        grid=(b, nq),
        in_specs=[
            pl.BlockSpec((1, MLA_HEADS, tq, MLA_KDIM), lambda i, j: (i, 0, j, 0)),
            pl.BlockSpec((1, lk, MLA_KDIM), lambda i, j: (i, 0, 0)),
            pl.BlockSpec((1, tq, d), lambda i, j: (i, j, 0)),
            pl.BlockSpec((1, 6, d), lambda i, j: (mod_row(i), 0, 0)),
            const(*w["w_uv"].shape),
            const(*w["w_o"].shape),
            const(1, d),
        ],
        out_specs=[flat, flat],
        out_shape=[jax.ShapeDtypeStruct((b * l, d), F32)] * 2,
        scratch_shapes=(
            [pltpu.VMEM((rows, 1), F32)] * 2 + [pltpu.VMEM((rows, MLA_KV_RANK), F32)] + [pltpu.VMEM((rows, tk), F32)] * 2
        ),
        compiler_params=_cparams("parallel", "arbitrary"),
        name="mla_attend",
    )(q, keys, x, mod, w["w_uv"], w["w_o"], n2)


def _router_kernel(h_ref, wt_ref, b_ref, idx_ref, gate_ref, rank_ref, cnt_ref, carry_sc, *, tile):
    @pl.when(pl.program_id(0) == 0)
    def _():
        carry_sc[...] = jnp.zeros(carry_sc.shape, F32)

    logits = lax.dot_general(wt_ref[...], h_ref[...], _NT, preferred_element_type=F32, precision=_HI) + b_ref[...]
    eid = lax.broadcasted_iota(jnp.int32, logits.shape, 0).astype(F32)
    vals, hots = [], []
    cur = logits
    for k in range(TOP_K):
        mx = jnp.max(cur, axis=0, keepdims=True)
        first = jnp.min(jnp.where(cur == mx, eid, float(N_EXPERTS)), axis=0, keepdims=True)
        hot = eid == first
        idx_ref[pl.ds(k, 1), :] = first.astype(jnp.int32)
        vals.append(mx)
        hots.append(hot)
        cur = jnp.where(hot, -jnp.inf, cur)
    ex = [jnp.exp(v - vals[0]) for v in vals]
    den = ex[0] + ex[1] + ex[2] + ex[3]
    for k in range(TOP_K):
        gate_ref[pl.ds(k, 1), :] = ex[k] / den
    cnt = sum(jnp.where(hot, 1.0, 0.0) for hot in hots)
    r = lax.broadcasted_iota(jnp.int32, (tile, tile), 0)
    c = lax.broadcasted_iota(jnp.int32, (tile, tile), 1)
    upper = jnp.where(r <= c, 1.0, 0.0).astype(BF16)
    incl = jnp.dot(cnt.astype(BF16), upper, preferred_element_type=F32)
    pos = carry_sc[...] + incl - 1.0
    for k in range(TOP_K):
        rank_ref[pl.ds(k, 1), :] = jnp.sum(jnp.where(hots[k], pos, 0.0), axis=0, keepdims=True).astype(jnp.int32)
    carry_sc[...] = carry_sc[...] + jnp.sum(cnt, axis=1, keepdims=True)
    cnt_ref[...] = jnp.broadcast_to(carry_sc[...], cnt_ref.shape)


def _route(h, w_router, b_router, tile):
    t, d = h.shape
    tile = min(tile, t)
    assert t % tile == 0
    tok = lambda i: (0, i)
    idx, gate, rank, cnt = pl.pallas_call(
        functools.partial(_router_kernel, tile=tile),
        grid=(t // tile,),
        in_specs=[
            pl.BlockSpec((tile, d), lambda i: (i, 0)),
            pl.BlockSpec((N_EXPERTS, d), lambda i: (0, 0)),
            pl.BlockSpec((N_EXPERTS, 1), lambda i: (0, 0)),
        ],
        out_specs=[
            pl.BlockSpec((TOP_K, tile), tok),
            pl.BlockSpec((TOP_K, tile), tok),
            pl.BlockSpec((TOP_K, tile), tok),
            pl.BlockSpec((N_EXPERTS, 128), lambda i: (0, 0)),
        ],
        out_shape=[
            jax.ShapeDtypeStruct((TOP_K, t), jnp.int32),
            jax.ShapeDtypeStruct((TOP_K, t), F32),
            jax.ShapeDtypeStruct((TOP_K, t), jnp.int32),
            jax.ShapeDtypeStruct((N_EXPERTS, 128), F32),
        ],
        scratch_shapes=[pltpu.VMEM((N_EXPERTS, 1), F32)],
        compiler_params=_cparams("arbitrary"),
        name="moe_router",
    )(h, w_router.T, b_router.reshape(N_EXPERTS, 1))
    return idx, gate, rank, cnt[:, 0].astype(jnp.int32)


SUBLANES = 8


def _fetch_rows_of_step(dest_hbm, dsm, isem, n):
    i = pl.program_id(0)
    slot = lax.rem(i, 2)

    def fetch(step, sl):
        return pltpu.make_async_copy(dest_hbm.at[pl.ds(pl.multiple_of(step * n, n), n)],
                                     dsm.at[pl.ds(pl.multiple_of(sl * n, n), n)], isem.at[sl])

    @pl.when(i == 0)
    def _():
        fetch(0, 0).start()

    @pl.when(i + 1 < pl.num_programs(0))
    def _():
        fetch(i + 1, 1 - slot).start()

    fetch(i, slot).wait()
    return slot * n


def _dispatch_kernel(endp_ref, dest_hbm, h_ref, xs_hbm, dsm, zbuf, isem, sem, zsem, *, tile, rows):
    i = pl.program_id(0)
    n = TOP_K * tile

    @pl.when(i == 0)
    def _():
        zbuf[...] = jnp.zeros(zbuf.shape, F32)

        def span(e):
            start = pl.multiple_of(jnp.maximum(endp_ref[e] - rows, 0), rows)
            return pltpu.make_async_copy(zbuf, xs_hbm.at[pl.ds(start, rows), :], zsem)

        for e in range(N_EXPERTS):
            span(e).start()
        for e in range(N_EXPERTS):
            span(e).wait()
        n_rows = xs_hbm.shape[0]
        for e in range(N_EXPERTS):
            start = pl.multiple_of(endp_ref[N_EXPERTS - 1] + e * rows, rows)

            @pl.when(start < n_rows)
            def _():
                tail = pltpu.make_async_copy(zbuf, xs_hbm.at[pl.ds(start, rows), :], zsem)
                tail.start()
                tail.wait()

    base = _fetch_rows_of_step(dest_hbm, dsm, isem, n)

    def issue(grp, carry):
        for u in range(SUBLANES):
            src = h_ref.at[grp, pl.ds(u, 1), :]
            for k in range(TOP_K):
                dst = xs_hbm.at[pl.ds(dsm[base + k * tile + grp * SUBLANES + u], 1), :]
                pltpu.make_async_copy(src, dst, sem).start(priority=k % 2)
        return carry

    lax.fori_loop(0, tile // SUBLANES, issue, 0)
    for k in range(TOP_K):
        pltpu.make_async_copy(h_ref, h_ref, sem).wait()


def _dispatch(end_p, dest_tiles, h, n_rows, tile, rows):
    t, d = h.shape
    s = SUBLANES
    assert tile % s == 0 and rows % s == 0
    xs = pl.pallas_call(
        functools.partial(_dispatch_kernel, tile=tile, rows=rows),
        grid_spec=pltpu.PrefetchScalarGridSpec(
            num_scalar_prefetch=1,
            grid=(t // tile,),
            in_specs=[pl.BlockSpec(memory_space=pl.ANY), pl.BlockSpec((tile // s, s, d), lambda i, ep: (i, 0, 0))],
            out_specs=pl.BlockSpec(memory_space=pl.ANY),
            scratch_shapes=[
                pltpu.SMEM((2 * TOP_K * tile,), jnp.int32),
                pltpu.VMEM((rows, d), F32),
                pltpu.SemaphoreType.DMA((2,)),
                pltpu.SemaphoreType.DMA,
                pltpu.SemaphoreType.DMA,
            ],
        ),
        out_shape=jax.ShapeDtypeStruct((n_rows, d), F32),
        compiler_params=_cparams("arbitrary"),
        name="moe_dispatch",
    )(end_p, dest_tiles, h.reshape(t // s, s, d))
    return xs


def _ffn_kernel(be_ref, nb_ref, xs_ref, wg_ref, wu_ref, bg_ref, bu_ref, wd_ref, bd_ref, y_ref):
    i = pl.program_id(0)

    @pl.when(i < nb_ref[0])
    def _():
        xb = xs_ref[...].astype(BF16)
        g = jnp.dot(xb, wg_ref[0], preferred_element_type=F32) + bg_ref[0]
        u = jnp.dot(xb, wu_ref[0], preferred_element_type=F32) + bu_ref[0]
        g = jnp.minimum(g, SWIGLU_LIMIT)
        u = jnp.clip(u, -SWIGLU_LIMIT, SWIGLU_LIMIT)
        a = (u + 1.0) * (g * jax.nn.sigmoid(SWIGLU_ALPHA * g))
        y_ref[...] = jnp.dot(a.astype(BF16), wd_ref[0, 0], preferred_element_type=F32) + bd_ref[0]

    @pl.when(i >= nb_ref[0])
    def _():
        y_ref[...] = jnp.zeros(y_ref.shape, F32)


def _expert_ffn(block_e, n_used, xs, w, rows):
    n_rows, d = xs.shape
    f = w["w_g"].shape[2]
    layer = w["layer"]
    ex = lambda i, be, nb: (be[i], 0, 0)
    return pl.pallas_call(
        _ffn_kernel,
        grid_spec=pltpu.PrefetchScalarGridSpec(
            num_scalar_prefetch=2,
            grid=(n_rows // rows,),
            in_specs=[
                pl.BlockSpec((rows, d), lambda i, be, nb: (jnp.minimum(i, nb[0] - 1), 0)),
                pl.BlockSpec((1, d, f), ex),
                pl.BlockSpec((1, d, f), ex),
                pl.BlockSpec((1, 1, f), ex),
                pl.BlockSpec((1, 1, f), ex),
                pl.BlockSpec((1, 1, f, d), lambda i, be, nb: (layer, be[i], 0, 0)),
                pl.BlockSpec((1, 1, d), ex),
            ],
            out_specs=pl.BlockSpec((rows, d), lambda i, be, nb: (i, 0)),
        ),
        out_shape=jax.ShapeDtypeStruct((n_rows, d), F32),
        compiler_params=_cparams("arbitrary"),
        name="moe_ffn",
    )(block_e, n_used, xs, w["w_g"], w["w_u"], w["b_g"], w["b_u"], w["w_d"], w["b_d"])


def _combine_kernel(dest_hbm, y_hbm, x_ref, gate_ref, mod_ref, fn_ref, o_ref, dsm, yg, isem, sem, *, tile, final):
    base = _fetch_rows_of_step(dest_hbm, dsm, isem, TOP_K * tile)

    def issue(grp, carry):
        for u in range(SUBLANES):
            for k in range(TOP_K):
                src = y_hbm.at[pl.ds(dsm[base + k * tile + grp * SUBLANES + u], 1), :]
                pltpu.make_async_copy(src, yg.at[k, grp, pl.ds(u, 1), :], sem).start(priority=k % 2)
        return carry

    lax.fori_loop(0, tile // SUBLANES, issue, 0)
    for k in range(TOP_K):
        pltpu.make_async_copy(yg.at[k], yg.at[k], sem).wait()
    gate = gate_ref[...]
    d = x_ref.shape[1]
    acc = yg[0].reshape(tile, d) * gate[:, 0:1]
    for k in range(1, TOP_K):
        acc = acc + yg[k].reshape(tile, d) * gate[:, k:k + 1]
    xn = x_ref[...] + _modrow(mod_ref, 5) * acc
    o_ref[...] = _rms(xn, fn_ref[...]) if final else xn


def _combine(dest_tiles, y, x, gate_t, mod, mod_row, final_norm, tile, final):
    t, d = x.shape
    s = SUBLANES
    return pl.pallas_call(
        functools.partial(_combine_kernel, tile=tile, final=final),
        grid=(t // tile,),
        in_specs=[
            pl.BlockSpec(memory_space=pl.ANY),
            pl.BlockSpec(memory_space=pl.ANY),
            pl.BlockSpec((tile, d), lambda i: (i, 0)),
            pl.BlockSpec((tile, TOP_K), lambda i: (i, 0)),
            pl.BlockSpec((1, 6, d), lambda i: (mod_row(i), 0, 0)),
            pl.BlockSpec((1, d), lambda i: (0, 0)),
        ],
        out_specs=pl.BlockSpec((tile, d), lambda i: (i, 0)),
        out_shape=jax.ShapeDtypeStruct((t, d), F32),
        scratch_shapes=[
            pltpu.SMEM((2 * TOP_K * tile,), jnp.int32),
            pltpu.VMEM((TOP_K, tile // s, s, d), F32),
            pltpu.SemaphoreType.DMA((2,)),
            pltpu.SemaphoreType.DMA,
        ],
        compiler_params=_cparams("arbitrary"),
        name="moe_combine",
    )(dest_tiles, y, x, gate_t, mod, final_norm)


def _moe(h, x, mod, mod_row_of_token, w, final_norm, final):
    t, d = h.shape
    rows = MOE_ROWS
    tile = min(COPY_TILE, t)
    assert t % tile == 0 and (TOP_K * tile) % 1024 == 0
    idx, gate, rank, counts = _route(h, w["w_router"], w["b_router"], ROUTER_TILE)
    padded = (counts + rows - 1) // rows * rows
    end_p = jnp.cumsum(padded)
    start_p = end_p - padded
    experts = jnp.arange(N_EXPERTS, dtype=jnp.int32)
    dest = jnp.sum(jnp.where(idx[:, :, None] == experts, start_p.astype(jnp.int32), 0), axis=-1) + rank
    n_rows = t * TOP_K + N_EXPERTS * rows
    n_blocks = n_rows // rows
    block_start = jnp.arange(n_blocks, dtype=jnp.int32) * rows
    block_e = jnp.minimum(jnp.sum(end_p[None, :] <= block_start[:, None], axis=1), N_EXPERTS - 1).astype(jnp.int32)
    n_used = (end_p[-1] // rows).astype(jnp.int32).reshape(1)
    dest_tiles = dest.reshape(TOP_K, t // tile, tile).transpose(1, 0, 2).reshape(-1)
    xs = _dispatch(end_p.astype(jnp.int32), dest_tiles, h, n_rows, tile, rows)
    y = _expert_ffn(block_e, n_used, xs, w, rows)
    return _combine(dest_tiles, y, x, gate.T, mod, lambda i: mod_row_of_token(i * tile), final_norm, tile, final)


def _log_sigmoid(x):
    return jnp.minimum(x, 0.0) - jnp.log1p(jnp.exp(-jnp.abs(x)))


def _gla_proj_kernel(x_ref, mod_ref, n1_ref, win_ref, wg1_ref, wg2_ref, bg_ref, q_ref, k_ref, v_ref, r_ref, g_ref,
                     *, nqk, nv, qscale):
    h = _rms(x_ref[...], n1_ref[...]) * (1.0 + _modrow(mod_ref, 1)) + _modrow(mod_ref, 0)
    hb = h.astype(BF16)
    a = jnp.dot(hb, win_ref[...], preferred_element_type=F32)
    q_ref[...] = a[:, :nqk] * qscale
    k_ref[...] = a[:, nqk:2 * nqk]
    v_ref[...] = a[:, 2 * nqk:2 * nqk + nv].astype(BF16)
    r_ref[...] = a[:, 2 * nqk + nv:]
    lo = jnp.dot(hb, wg1_ref[...], preferred_element_type=F32)
    for dr in range(2):
        z = jnp.dot(lo[:, dr * GLA_GATE_RANK:(dr + 1) * GLA_GATE_RANK], wg2_ref[dr],
                    preferred_element_type=F32, precision=_HI) + bg_ref[dr]
        g_ref[dr] = _log_sigmoid(z) * float(np.log2(np.e) / GLA_GATE_NORM)


def _gla_project(x, mod, mod_row_of_token, n1, w, tile):
    t, d = x.shape
    tile = min(tile, t)
    assert t % tile == 0
    nqk = w["w_gate2"].shape[2]
    nv = (w["w_in"].shape[1] - 2 * nqk) // 2
    dk = nqk // GLA_HEADS
    const = lambda *shape: pl.BlockSpec(shape, lambda i: (0,) * len(shape))
    row = lambda n: pl.BlockSpec((tile, n), lambda i: (i, 0))
    return pl.pallas_call(
        functools.partial(_gla_proj_kernel, nqk=nqk, nv=nv, qscale=float(dk ** -0.5)),
        grid=(t // tile,),
        in_specs=[
            row(d),
            pl.BlockSpec((1, 6, d), lambda i: (mod_row_of_token(i * tile), 0, 0)),
            const(1, d),
            const(*w["w_in"].shape),
            const(*w["w_gate1"].shape),
            const(*w["w_gate2"].shape),
            const(*w["b_gate"].shape),
        ],
        out_specs=[row(nqk), row(nqk), row(nv), row(nv), pl.BlockSpec((2, tile, nqk), lambda i: (0, i, 0))],
        out_shape=[
            jax.ShapeDtypeStruct((t, nqk), F32),
            jax.ShapeDtypeStruct((t, nqk), F32),
            jax.ShapeDtypeStruct((t, nv), BF16),
            jax.ShapeDtypeStruct((t, nv), F32),
            jax.ShapeDtypeStruct((2, t, nqk), F32),
        ],
        compiler_params=_cparams("parallel"),
        name="gla_project",
    )(x, mod, n1, w["w_in"], w["w_gate1"], w["w_gate2"], w["b_gate"])


def _gla_level_tables(chunk):
    n_lvl = int(np.log2(chunk))
    i = np.arange(chunk)[:, None]
    j = np.arange(chunk)[None, :]
    level = np.full((chunk, chunk), -1, np.int32)
    for lv in range(n_lvl):
        s = chunk >> (lv + 1)
        same = (i // (2 * s)) == (j // (2 * s))
        level[same & ((i % (2 * s)) >= s) & ((j % (2 * s)) < s)] = lv
    level[i == j] = n_lvl
    lower = (j <= i)
    cum = np.stack([lower, lower.T]).astype(np.float32)
    return np.stack([level, level.T]), np.concatenate([cum, cum], axis=2)


def _ref_rows(b, s):
    c, n = b.shape
    if s >= 4:
        return jnp.broadcast_to(b.reshape(c // (2 * s), 2 * s, n)[:, s:s + 1, :], (c // (2 * s), 2 * s, n)).reshape(c, n)
    pos = lax.broadcasted_iota(jnp.int32, (c, n), 0) % (2 * s)
    out = b
    for p in range(2 * s):
        if p != s:
            out = jnp.where(pos == p, pltpu.roll(b, (p - s) % c, axis=0), out)
    return out


def _gla_scan_kernel(q_ref, k_ref, v_ref, g_ref, lvl_ref, cum_ref, o_ref, st_sc, *, chunk, dk, dv, nch):
    dr = pl.program_id(1)

    @pl.when(pl.program_id(2) == 0)
    def _():
        st_sc[...] = jnp.zeros(st_sc.shape, F32)

    n_lvl = int(np.log2(chunk))
    lvl = lvl_ref[0]
    cum = cum_ref[0].astype(BF16)
    for ci in range(nch):
        c_eff = jnp.where(dr == 0, ci, nch - 1 - ci)
        rows = pl.ds(pl.multiple_of(c_eff * chunk, chunk), chunk)
        for hh in range(GLA_HEADS):
            qs = q_ref[rows, hh * dk:(hh + 1) * dk]
            ks = k_ref[rows, hh * dk:(hh + 1) * dk]
            vs = v_ref[rows, hh * dv:(hh + 1) * dv].astype(BF16)
            g = g_ref[0, rows, hh * dk:(hh + 1) * dk]
            g_hi = g.astype(BF16)
            g_lo = (g - g_hi.astype(F32)).astype(BF16)
            b = jnp.dot(cum, jnp.concatenate([g_hi, g_lo], axis=0), preferred_element_type=F32)
            b_tot = jnp.sum(g, axis=0, keepdims=True)
            st = st_sc[hh]
            q_in = (qs * jnp.exp2(b)).astype(BF16)
            o = lax.dot_general(q_in, st.astype(BF16), _NT, preferred_element_type=F32)
            qb = qs.astype(BF16)
            kb = ks.astype(BF16)
            a = jnp.where(lvl == n_lvl, lax.dot_general(qb, kb, _NT, preferred_element_type=F32), 0.0)
            for lv in range(n_lvl):
                e = jnp.exp2(-jnp.abs(b - _ref_rows(b, chunk >> (lv + 1)))).astype(BF16)
                al = lax.dot_general(qb * e, kb * e, _NT, preferred_element_type=F32)
                a = jnp.where(lvl == lv, al, a)
            o = o + jnp.dot(a.astype(BF16), vs, preferred_element_type=F32)
            o_ref[0, rows, hh * dv:(hh + 1) * dv] = o
            k_out = (ks * jnp.exp2(b_tot - b)).astype(BF16)
            st_sc[hh] = st * jnp.exp2(b_tot) + lax.dot_general(vs, k_out, _TN, preferred_element_type=F32)


def _gla_scan(q, k, v, g, batch, n_lat, n_ctx):
    t, nqk = q.shape
    nv = v.shape[1]
    blk = GLA_BLOCK
    assert n_ctx == blk and n_lat % blk == 0
    nlat = n_lat // blk
    ctx0 = batch * nlat
    chunk = GLA_CHUNK
    lvl, cum = _gla_level_tables(chunk)

    def rowblk(bb, dr, j):
        lat = bb * nlat + jnp.where(dr == 0, j - 1, nlat - j)
        return jnp.where(j == 0, ctx0 + bb, lat)

    tok = lambda n: pl.BlockSpec((blk, n), lambda bb, dr, j: (rowblk(bb, dr, j), 0))
    return pl.pallas_call(
        functools.partial(_gla_scan_kernel, chunk=chunk, dk=nqk // GLA_HEADS, dv=nv // GLA_HEADS, nch=blk // chunk),
        grid=(batch, 2, nlat + 1),
        in_specs=[
            tok(nqk), tok(nqk), tok(nv),
            pl.BlockSpec((1, blk, nqk), lambda bb, dr, j: (dr, rowblk(bb, dr, j), 0)),
            pl.BlockSpec((1, chunk, chunk), lambda bb, dr, j: (dr, 0, 0)),
            pl.BlockSpec((1, chunk, 2 * chunk), lambda bb, dr, j: (dr, 0, 0)),
        ],
        out_specs=pl.BlockSpec((1, blk, nv), lambda bb, dr, j: (dr, rowblk(bb, dr, j), 0)),
        out_shape=jax.ShapeDtypeStruct((2, t, nv), F32),
        scratch_shapes=[pltpu.VMEM((GLA_HEADS, nv // GLA_HEADS, nqk // GLA_HEADS), F32)],
        compiler_params=_cparams("parallel", "parallel", "arbitrary"),
        name="gla_scan",
    )(q, k, v, g, jnp.asarray(lvl), jnp.asarray(cum))


def _gla_finish_kernel(o_ref, r_ref, x_ref, mod_ref, hn_ref, wo_ref, n2_ref, xo_ref, h2_ref, *, dv):
    o = o_ref[0] + o_ref[1]
    r = r_ref[...]
    parts = []
    for hh in range(GLA_HEADS):
        sl = slice(hh * dv, (hh + 1) * dv)
        parts.append(_rms(o[:, sl], hn_ref[...]) * (r[:, sl] * jax.nn.sigmoid(r[:, sl])))
    gated = jnp.concatenate(parts, axis=-1).astype(BF16)
    y = jnp.dot(gated, wo_ref[...], preferred_element_type=F32)
    xn = x_ref[...] + _modrow(mod_ref, 2) * y
    xo_ref[...] = xn
    h2_ref[...] = _rms(xn, n2_ref[...]) * (1.0 + _modrow(mod_ref, 4)) + _modrow(mod_ref, 3)


def _gla_finish(o, r, x, t, mod, mod_row_of_token, w, n2, tile):
    d = x.shape[1]
    nv = r.shape[1]
    tile = min(tile, t)
    assert t % tile == 0
    const = lambda *shape: pl.BlockSpec(shape, lambda i: (0,) * len(shape))
    row = lambda n: pl.BlockSpec((tile, n), lambda i: (i, 0))
    return pl.pallas_call(
        functools.partial(_gla_finish_kernel, dv=nv // GLA_HEADS),
        grid=(t // tile,),
        in_specs=[
            pl.BlockSpec((2, tile, nv), lambda i: (0, i, 0)),
            row(nv), row(d),
            pl.BlockSpec((1, 6, d), lambda i: (mod_row_of_token(i * tile), 0, 0)),
            const(1, nv // GLA_HEADS),
            const(nv, d),
            const(1, d),
        ],
        out_specs=[row(d), row(d)],
        out_shape=[jax.ShapeDtypeStruct((t, d), F32), jax.ShapeDtypeStruct((t, d), F32)],
        compiler_params=_cparams("parallel"),
        name="gla_finish",
    )(o, r, x, mod, w["head_norm"], w["w_o"], n2)


def _rope_tables(n):
    rows = n // GRID_W
    row = jnp.repeat(jnp.arange(rows, dtype=F32), GRID_W)
    col = jnp.tile(jnp.arange(GRID_W, dtype=F32), rows)
    nf = MLA_ROPE // 4
    inv = jnp.power(ROPE_BASE, -jnp.arange(nf, dtype=F32) / nf)
    ang = jnp.concatenate([row[:, None] * inv, col[:, None] * inv], axis=-1)
    cos, sin = jnp.cos(ang), jnp.sin(ang)
    cr, cc, sr, sc = cos[:, :nf], cos[:, nf:], sin[:, :nf], sin[:, nf:]
    cos64 = jnp.concatenate([cr, cr, cc, cc], axis=-1)
    sin64 = jnp.concatenate([-sr, sr, -sc, sc], axis=-1)
    return jnp.tile(cos64, (1, MLA_HEADS)), jnp.tile(sin64, (1, MLA_HEADS))


def _swap_halves(w):
    nf = MLA_ROPE // 4
    perm = np.concatenate([np.arange(nf, 2 * nf), np.arange(0, nf), np.arange(3 * nf, 4 * nf), np.arange(2 * nf, 3 * nf)])
    return w[..., perm]


def _mla_weights(w_in, q_norm, w_uq, kv_norm, w_ukv, w_o):
    hh = MLA_HEADS
    o = MLA_Q_RANK + MLA_KV_RANK
    w_in_ext = jnp.concatenate([w_in, _swap_halves(w_in[:, o:])], axis=1)
    uq = w_uq.reshape(MLA_Q_RANK, hh, MLA_NOPE + MLA_ROPE)
    w_uk = w_ukv[:, :hh * MLA_NOPE].reshape(MLA_KV_RANK, hh, MLA_NOPE)
    w_uv = w_ukv[:, hh * MLA_NOPE:].reshape(MLA_KV_RANK, hh, MLA_V)
    return {
        "w_in": w_in_ext.astype(BF16),
        "q_norm": q_norm.reshape(1, -1),
        "kv_norm": kv_norm.reshape(1, -1),
        "w_qn": uq[:, :, :MLA_NOPE].reshape(MLA_Q_RANK, hh * MLA_NOPE).astype(BF16),
        "w_qr": uq[:, :, MLA_NOPE:].reshape(MLA_Q_RANK, hh * MLA_ROPE).astype(BF16),
        "w_qs": _swap_halves(uq[:, :, MLA_NOPE:]).reshape(MLA_Q_RANK, hh * MLA_ROPE).astype(BF16),
        "w_ukT": jnp.transpose(w_uk, (1, 2, 0)).astype(BF16),
        "w_uv": jnp.transpose(w_uv, (1, 0, 2)).astype(BF16),
        "w_o": w_o.astype(BF16),
    }


def _split_gu_kernel(w_ref, g_ref, u_ref, *, cols):
    r = lax.broadcasted_iota(jnp.int32, (cols, cols), 0)
    c = lax.broadcasted_iota(jnp.int32, (cols, cols), 1)
    src = jnp.where(c < cols // 2, 2 * c, 2 * (c - cols // 2) + 1)
    perm = jnp.where(r == src, 1.0, 0.0).astype(BF16)
    out = jnp.dot(w_ref[0, 0].astype(BF16), perm, preferred_element_type=F32).astype(BF16)
    g_ref[0] = out[:, :cols // 2]
    u_ref[0] = out[:, cols // 2:]


def _split_gate_up(w_gu, layer):
    _, e, d, f2 = w_gu.shape
    cols = 512
    half = pl.BlockSpec((1, d, cols // 2), lambda i, j: (i, 0, j))
    return pl.pallas_call(
        functools.partial(_split_gu_kernel, cols=cols),
        grid=(e, f2 // cols),
        in_specs=[pl.BlockSpec((1, 1, d, cols), lambda i, j: (layer, i, 0, j))],
        out_specs=[half, half],
        out_shape=[jax.ShapeDtypeStruct((e, d, f2 // 2), BF16)] * 2,
        compiler_params=_cparams("parallel", "parallel"),
        name="moe_split_gate_up",
    )(w_gu)


def _moe_weights(layer, w_router, b_router, w_gu, b_gu, w_down_bf16, b_down):
    e = w_gu.shape[1]
    w_g, w_u = _split_gate_up(w_gu, layer)
    return {
        "layer": layer,
        "w_router": w_router[layer],
        "b_router": b_router[layer],
        "w_g": w_g,
        "w_u": w_u,
        "b_g": b_gu[layer, :, 0::2].reshape(e, 1, -1),
        "b_u": b_gu[layer, :, 1::2].reshape(e, 1, -1),
        "w_d": w_down_bf16,
        "b_d": b_down[layer].reshape(e, 1, -1),
    }


def kernel(x, c, ctx, c_ctx, mod_w, mod_b, norm1, norm2, mla_w_in, mla_q_norm, mla_w_uq, mla_kv_norm, mla_w_ukv, mla_w_o, gla_w_in, gla_w_gate1, gla_w_gate2, gla_b_gate, gla_head_norm, gla_w_o, moe_w_router, moe_b_router, moe_w_gu, moe_b_gu, moe_w_down, moe_b_down, final_norm):
    B, N, D = x.shape
    Lc = ctx.shape[1]
    assert B < 8
    cc = jnp.zeros((8, D), F32).at[:B].set(c).at[B].set(c_ctx)
    mod = _modulation(cc, mod_w, mod_b).reshape(mod_w.shape[0], 8, 6, D)
    row_of_token = lambda tok: jnp.minimum(tok // N, B)
    n1 = norm1.reshape(-1, 1, D)
    n2 = norm2.reshape(-1, 1, D)
    fnorm = final_norm.reshape(1, D)

    mw = _mla_weights(mla_w_in[0], mla_q_norm[0], mla_w_uq[0], mla_kv_norm[0], mla_w_ukv[0], mla_w_o[0])
    cos, sin = _rope_tables(N)
    ones = jnp.ones((Lc, cos.shape[1]), F32)
    q_l, k_l = _mla_project(x, mod[0], lambda b: b, n1[0], mw, cos, sin, TOKEN_TILE)
    q_c, k_c = _mla_project(ctx, mod[0], lambda b: B, n1[0], mw, ones, jnp.zeros_like(ones), TOKEN_TILE)
    k_all = jnp.concatenate([k_l, k_c], axis=1)
    x1, h1 = _mla_attend(q_l, k_all, x, mod[0], lambda b: b, n2[0], mw, ATTN_Q_TILE, ATTN_KV_TILE)
    xc1, hc1 = _mla_attend(q_c, k_c, ctx, mod[0], lambda b: B, n2[0], mw, ATTN_Q_TILE, ATTN_KV_TILE)
    x_all = jnp.concatenate([x1, xc1], axis=0)
    h_all = jnp.concatenate([h1, hc1], axis=0)
    w_down_bf16 = moe_w_down.astype(BF16)
    ew = _moe_weights(0, moe_w_router, moe_b_router, moe_w_gu, moe_b_gu, w_down_bf16, moe_b_down)
    x_all = _moe(h_all, x_all, mod[0], row_of_token, ew, fnorm, final=False)

    gw = {
        "w_in": gla_w_in[0].astype(BF16),
        "w_gate1": jnp.concatenate([gla_w_gate1[0, 0], gla_w_gate1[0, 1]], axis=1).astype(BF16),
        "w_gate2": gla_w_gate2[0],
        "b_gate": gla_b_gate[0].reshape(2, 1, -1),
        "head_norm": gla_head_norm[0].reshape(1, -1),
        "w_o": gla_w_o[0].astype(BF16),
    }
    q, k, v, r, g = _gla_project(x_all, mod[1], row_of_token, n1[1], gw, TOKEN_TILE)
    o = _gla_scan(q, k, v, g, B, N, Lc)
    x2, h2 = _gla_finish(o, r, x_all, B * N, mod[1], row_of_token, gw, n2[1], TOKEN_TILE)
    ew = _moe_weights(1, moe_w_router, moe_b_router, moe_w_gu, moe_b_gu, w_down_bf16, moe_b_down)
    out = _moe(h2, x2, mod[1], row_of_token, ew, fnorm, final=True)
    return out.reshape(B, N, D)
```

```python
import functools

import numpy as np
import jax
import jax.numpy as jnp
from jax import lax
from jax.experimental import pallas as pl
from jax.experimental.pallas import tpu as pltpu

F32 = jnp.float32
BF16 = jnp.bfloat16
EPS = 1e-6

GRID_W = 64
ROPE_BASE = 10000.0
MLA_HEADS = 8
MLA_Q_RANK = 512
MLA_KV_RANK = 256
MLA_NOPE = 128
MLA_ROPE = 64
MLA_V = 128
MLA_KDIM = MLA_KV_RANK + MLA_ROPE
GLA_HEADS = 4
GLA_GATE_RANK = 16
GLA_GATE_NORM = 16.0
GLA_CHUNK = 128
N_EXPERTS = 32
TOP_K = 4
SWIGLU_ALPHA = 1.702
SWIGLU_LIMIT = 7.0

VMEM_LIMIT_BYTES = 56 * 1024 * 1024
MOE_ROWS = 512
TOKEN_TILE = 512
ATTN_Q_TILE = 128
ATTN_KV_TILE = 768
ROUTER_TILE = 512
COPY_TILE = 512
GLA_BLOCK = 256

_HI = lax.Precision.HIGHEST
_NT = (((1,), (1,)), ((), ()))
_TN = (((0,), (0,)), ((), ()))


def _cparams(*sem):
    return pltpu.CompilerParams(dimension_semantics=sem, vmem_limit_bytes=VMEM_LIMIT_BYTES)


def _rms(x, g):
    return x * lax.rsqrt(jnp.mean(x * x, axis=-1, keepdims=True) + EPS) * g


def _modrow(mod_ref, k):
    return mod_ref[0, pl.ds(k, 1), :]


def _mod_kernel(cc_ref, w_ref, b_ref, o_ref):
    cc = cc_ref[...]
    s = cc * jax.nn.sigmoid(cc)
    o_ref[0] = jnp.dot(s, w_ref[0], preferred_element_type=F32, precision=_HI) + b_ref[0]


def _modulation(cc, mod_w, mod_b):
    depth, d, d6 = mod_w.shape
    tn = 1536
    return pl.pallas_call(
        _mod_kernel,
        grid=(depth, d6 // tn),
        in_specs=[
            pl.BlockSpec((8, d), lambda i, j: (0, 0)),
            pl.BlockSpec((1, d, tn), lambda i, j: (i, 0, j)),
            pl.BlockSpec((1, 1, tn), lambda i, j: (i, 0, j)),
        ],
        out_specs=pl.BlockSpec((1, 8, tn), lambda i, j: (i, 0, j)),
        out_shape=jax.ShapeDtypeStruct((depth, 8, d6), F32),
        compiler_params=_cparams("parallel", "parallel"),
        name="modulation",
    )(cc, mod_w, mod_b.reshape(depth, 1, d6))


def _mla_proj_kernel(x_ref, mod_ref, n1_ref, win_ref, qn_ref, kvn_ref, wqn_ref, wqr_ref, wqs_ref, wuk_ref,
                     cos_ref, sin_ref, q_ref, k_ref, *, scale):
    x = x_ref[0]
    h = _rms(x, n1_ref[...]) * (1.0 + _modrow(mod_ref, 1)) + _modrow(mod_ref, 0)
    a = jnp.dot(h.astype(BF16), win_ref[...], preferred_element_type=F32)
    cq = _rms(a[:, :MLA_Q_RANK], qn_ref[...]).astype(BF16)
    ckv = _rms(a[:, MLA_Q_RANK:MLA_Q_RANK + MLA_KV_RANK], kvn_ref[...])
    cos = cos_ref[...]
    sin = sin_ref[...]
    o = MLA_Q_RANK + MLA_KV_RANK
    kr = a[:, o:o + MLA_ROPE] * cos[:, :MLA_ROPE] + a[:, o + MLA_ROPE:o + 2 * MLA_ROPE] * sin[:, :MLA_ROPE]
    k_ref[0, :, :MLA_KV_RANK] = ckv.astype(BF16)
    k_ref[0, :, MLA_KV_RANK:] = kr.astype(BF16)
    qn = jnp.dot(cq, wqn_ref[...], preferred_element_type=F32)
    qr = (jnp.dot(cq, wqr_ref[...], preferred_element_type=F32) * cos
          + jnp.dot(cq, wqs_ref[...], preferred_element_type=F32) * sin)
    for hh in range(MLA_HEADS):
        qe = jnp.dot(qn[:, hh * MLA_NOPE:(hh + 1) * MLA_NOPE].astype(BF16), wuk_ref[hh], preferred_element_type=F32)
        q_ref[0, hh, :, :MLA_KV_RANK] = (qe * scale).astype(BF16)
        q_ref[0, hh, :, MLA_KV_RANK:] = (qr[:, hh * MLA_ROPE:(hh + 1) * MLA_ROPE] * scale).astype(BF16)


def _mla_project(x, mod, mod_row, n1, w, cos, sin, tile):
    b, l, d = x.shape
    tile = min(tile, l)
    assert l % tile == 0
    scale = float((MLA_NOPE + MLA_ROPE) ** -0.5 * np.log2(np.e))
    const = lambda *shape: pl.BlockSpec(shape, lambda i, j: (0,) * len(shape))
    return pl.pallas_call(
        functools.partial(_mla_proj_kernel, scale=scale),
        grid=(b, l // tile),
        in_specs=[
            pl.BlockSpec((1, tile, d), lambda i, j: (i, j, 0)),
            pl.BlockSpec((1, 6, d), lambda i, j: (mod_row(i), 0, 0)),
            const(1, d),
            const(*w["w_in"].shape),
            const(1, MLA_Q_RANK),
            const(1, MLA_KV_RANK),
            const(*w["w_qn"].shape),
            const(*w["w_qr"].shape),
            const(*w["w_qs"].shape),
            const(*w["w_ukT"].shape),
            pl.BlockSpec((tile, MLA_HEADS * MLA_ROPE), lambda i, j: (j, 0)),
            pl.BlockSpec((tile, MLA_HEADS * MLA_ROPE), lambda i, j: (j, 0)),
        ],
        out_specs=[
            pl.BlockSpec((1, MLA_HEADS, tile, MLA_KDIM), lambda i, j: (i, 0, j, 0)),
            pl.BlockSpec((1, tile, MLA_KDIM), lambda i, j: (i, j, 0)),
        ],
        out_shape=[
            jax.ShapeDtypeStruct((b, MLA_HEADS, l, MLA_KDIM), BF16),
            jax.ShapeDtypeStruct((b, l, MLA_KDIM), BF16),
        ],
        compiler_params=_cparams("parallel", "parallel"),
        name="mla_project",
    )(x, mod, n1, w["w_in"], w["q_norm"], w["kv_norm"], w["w_qn"], w["w_qr"], w["w_qs"], w["w_ukT"], cos, sin)


def _attn_kernel(q_ref, k_ref, x_ref, mod_ref, wuv_ref, wo_ref, n2_ref, xo_ref, h2_ref, m_sc, l_sc, acc_sc, *s_sc,
                 tq, tk, nkv):
    rows = MLA_HEADS * tq
    m_sc[...] = jnp.full(m_sc.shape, -jnp.inf, F32)
    l_sc[...] = jnp.zeros(l_sc.shape, F32)
    acc_sc[...] = jnp.zeros(acc_sc.shape, F32)

    def keys(c):
        return k_ref[0, pl.ds(pl.multiple_of(c * tk, tk), tk), :]

    def scores(c, slot):
        s_sc[slot][...] = lax.dot_general(q_ref[0].reshape(rows, MLA_KDIM), keys(c), _NT, preferred_element_type=F32)

    def absorb(c, slot):
        v = keys(c)[:, :MLA_KV_RANK]
        s = s_sc[slot][...]
        m_prev = m_sc[...]
        m_new = jnp.maximum(m_prev, jnp.max(s, axis=-1, keepdims=True))
        alpha = jnp.exp2(m_prev - m_new)
        p = jnp.exp2(s - m_new)
        l_sc[...] = alpha * l_sc[...] + jnp.sum(p, axis=-1, keepdims=True)
        acc_sc[...] = alpha * acc_sc[...] + jnp.dot(p.astype(BF16), v, preferred_element_type=F32)
        m_sc[...] = m_new

    def phase(c, slot, has_next):
        if has_next:
            scores(c + 1, 1 - slot)
        absorb(c, slot)

    scores(0, 0)
    npairs = (nkv - 1) // 2

    def body(j, carry):
        phase(2 * j, 0, True)
        phase(2 * j + 1, 1, True)
        return carry

    lax.fori_loop(0, npairs, body, 0)
    if nkv - 2 * npairs == 2:
        phase(nkv - 2, 0, True)
        phase(nkv - 1, 1, False)
    else:
        phase(nkv - 1, 0, False)
    u = []
    for hh in range(MLA_HEADS):
        o = (acc_sc[hh * tq:(hh + 1) * tq] / l_sc[hh * tq:(hh + 1) * tq]).astype(BF16)
        u.append(jnp.dot(o, wuv_ref[hh], preferred_element_type=F32))
    u = jnp.concatenate(u, axis=-1).astype(BF16)
    y = jnp.dot(u, wo_ref[...], preferred_element_type=F32)
    xn = x_ref[0] + _modrow(mod_ref, 2) * y
    xo_ref[...] = xn
    h2_ref[...] = _rms(xn, n2_ref[...]) * (1.0 + _modrow(mod_ref, 4)) + _modrow(mod_ref, 3)


def _mla_attend(q, keys, x, mod, mod_row, n2, w, tq, tk):
    b, l, d = x.shape
    lk = keys.shape[1]
    tq = min(tq, l)
    tk = min(tk, lk)
    assert lk % tk == 0 and l % tq == 0
    rows = MLA_HEADS * tq
    nq = l // tq
    const = lambda *shape: pl.BlockSpec(shape, lambda i, j: (0,) * len(shape))
    flat = pl.BlockSpec((tq, d), lambda i, j: (i * nq + j, 0))
    return pl.pallas_call(
        functools.partial(_attn_kernel, tq=tq, tk=tk, nkv=lk // tk),
        grid=(b, nq),
        in_specs=[
            pl.BlockSpec((1, MLA_HEADS, tq, MLA_KDIM), lambda i, j: (i, 0, j, 0)),
            pl.BlockSpec((1, lk, MLA_KDIM), lambda i, j: (i, 0, 0)),
            pl.BlockSpec((1, tq, d), lambda i, j: (i, j, 0)),
            pl.BlockSpec((1, 6, d), lambda i, j: (mod_row(i), 0, 0)),
            const(*w["w_uv"].shape),
            const(*w["w_o"].shape),
            const(1, d),
        ],
        out_specs=[flat, flat],
        out_shape=[jax.ShapeDtypeStruct((b * l, d), F32)] * 2,
        scratch_shapes=(
            [pltpu.VMEM((rows, 1), F32)] * 2 + [pltpu.VMEM((rows, MLA_KV_RANK), F32)] + [pltpu.VMEM((rows, tk), F32)] * 2
        ),
        compiler_params=_cparams("parallel", "arbitrary"),
        name="mla_attend",
    )(q, keys, x, mod, w["w_uv"], w["w_o"], n2)


def _router_kernel(h_ref, wt_ref, b_ref, idx_ref, gate_ref, rank_ref, cnt_ref, carry_sc, *, tile):
    @pl.when(pl.program_id(0) == 0)
    def _():
        carry_sc[...] = jnp.zeros(carry_sc.shape, F32)

    logits = lax.dot_general(wt_ref[...], h_ref[...], _NT, preferred_element_type=F32, precision=_HI) + b_ref[...]
    eid = lax.broadcasted_iota(jnp.int32, logits.shape, 0).astype(F32)
    vals, hots = [], []
    cur = logits
    for k in range(TOP_K):
        mx = jnp.max(cur, axis=0, keepdims=True)
        first = jnp.min(jnp.where(cur == mx, eid, float(N_EXPERTS)), axis=0, keepdims=True)
        hot = eid == first
        idx_ref[pl.ds(k, 1), :] = first.astype(jnp.int32)
        vals.append(mx)
        hots.append(hot)
        cur = jnp.where(hot, -jnp.inf, cur)
    ex = [jnp.exp(v - vals[0]) for v in vals]
    den = ex[0] + ex[1] + ex[2] + ex[3]
    for k in range(TOP_K):
        gate_ref[pl.ds(k, 1), :] = ex[k] / den
    cnt = sum(jnp.where(hot, 1.0, 0.0) for hot in hots)
    r = lax.broadcasted_iota(jnp.int32, (tile, tile), 0)
    c = lax.broadcasted_iota(jnp.int32, (tile, tile), 1)
    upper = jnp.where(r <= c, 1.0, 0.0).astype(BF16)
    incl = jnp.dot(cnt.astype(BF16), upper, preferred_element_type=F32)
    pos = carry_sc[...] + incl - 1.0
    for k in range(TOP_K):
        rank_ref[pl.ds(k, 1), :] = jnp.sum(jnp.where(hots[k], pos, 0.0), axis=0, keepdims=True).astype(jnp.int32)
    carry_sc[...] = carry_sc[...] + jnp.sum(cnt, axis=1, keepdims=True)
    cnt_ref[...] = jnp.broadcast_to(carry_sc[...], cnt_ref.shape)


def _route(h, w_router, b_router, tile):
    t, d = h.shape
    tile = min(tile, t)
    assert t % tile == 0
    tok = lambda i: (0, i)
    idx, gate, rank, cnt = pl.pallas_call(
        functools.partial(_router_kernel, tile=tile),
        grid=(t // tile,),
        in_specs=[
            pl.BlockSpec((tile, d), lambda i: (i, 0)),
            pl.BlockSpec((N_EXPERTS, d), lambda i: (0, 0)),
            pl.BlockSpec((N_EXPERTS, 1), lambda i: (0, 0)),
        ],
        out_specs=[
            pl.BlockSpec((TOP_K, tile), tok),
            pl.BlockSpec((TOP_K, tile), tok),
            pl.BlockSpec((TOP_K, tile), tok),
            pl.BlockSpec((N_EXPERTS, 128), lambda i: (0, 0)),
        ],
        out_shape=[
            jax.ShapeDtypeStruct((TOP_K, t), jnp.int32),
            jax.ShapeDtypeStruct((TOP_K, t), F32),
            jax.ShapeDtypeStruct((TOP_K, t), jnp.int32),
            jax.ShapeDtypeStruct((N_EXPERTS, 128), F32),
        ],
        scratch_shapes=[pltpu.VMEM((N_EXPERTS, 1), F32)],
        compiler_params=_cparams("arbitrary"),
        name="moe_router",
    )(h, w_router.T, b_router.reshape(N_EXPERTS, 1))
    return idx, gate, rank, cnt[:, 0].astype(jnp.int32)


SUBLANES = 8


def _fetch_rows_of_step(dest_hbm, dsm, isem, n):
    i = pl.program_id(0)
    slot = lax.rem(i, 2)

    def fetch(step, sl):
        return pltpu.make_async_copy(dest_hbm.at[pl.ds(pl.multiple_of(step * n, n), n)],
                                     dsm.at[pl.ds(pl.multiple_of(sl * n, n), n)], isem.at[sl])

    @pl.when(i == 0)
    def _():
        fetch(0, 0).start()

    @pl.when(i + 1 < pl.num_programs(0))
    def _():
        fetch(i + 1, 1 - slot).start()

    fetch(i, slot).wait()
    return slot * n


def _dispatch_kernel(endp_ref, dest_hbm, h_ref, xs_hbm, dsm, zbuf, isem, sem, zsem, *, tile, rows):
    i = pl.program_id(0)
    n = TOP_K * tile

    @pl.when(i == 0)
    def _():
        zbuf[...] = jnp.zeros(zbuf.shape, F32)

        def span(e):
            start = pl.multiple_of(jnp.maximum(endp_ref[e] - rows, 0), rows)
            return pltpu.make_async_copy(zbuf, xs_hbm.at[pl.ds(start, rows), :], zsem)

        for e in range(N_EXPERTS):
            span(e).start()
        for e in range(N_EXPERTS):
            span(e).wait()
        n_rows = xs_hbm.shape[0]
        for e in range(N_EXPERTS):
            start = pl.multiple_of(endp_ref[N_EXPERTS - 1] + e * rows, rows)

            @pl.when(start < n_rows)
            def _():
                tail = pltpu.make_async_copy(zbuf, xs_hbm.at[pl.ds(start, rows), :], zsem)
                tail.start()
                tail.wait()

    base = _fetch_rows_of_step(dest_hbm, dsm, isem, n)

    def issue(grp, carry):
        for u in range(SUBLANES):
            src = h_ref.at[grp, pl.ds(u, 1), :]
            for k in range(TOP_K):
                dst = xs_hbm.at[pl.ds(dsm[base + k * tile + grp * SUBLANES + u], 1), :]
                pltpu.make_async_copy(src, dst, sem).start(priority=k % 2)
        return carry

    lax.fori_loop(0, tile // SUBLANES, issue, 0)
    for k in range(TOP_K):
        pltpu.make_async_copy(h_ref, h_ref, sem).wait()


def _dispatch(end_p, dest_tiles, h, n_rows, tile, rows):
    t, d = h.shape
    s = SUBLANES
    assert tile % s == 0 and rows % s == 0
    xs = pl.pallas_call(
        functools.partial(_dispatch_kernel, tile=tile, rows=rows),
        grid_spec=pltpu.PrefetchScalarGridSpec(
            num_scalar_prefetch=1,
            grid=(t // tile,),
            in_specs=[pl.BlockSpec(memory_space=pl.ANY), pl.BlockSpec((tile // s, s, d), lambda i, ep: (i, 0, 0))],
            out_specs=pl.BlockSpec(memory_space=pl.ANY),
            scratch_shapes=[
                pltpu.SMEM((2 * TOP_K * tile,), jnp.int32),
                pltpu.VMEM((rows, d), F32),
                pltpu.SemaphoreType.DMA((2,)),
                pltpu.SemaphoreType.DMA,
                pltpu.SemaphoreType.DMA,
            ],
        ),
        out_shape=jax.ShapeDtypeStruct((n_rows, d), F32),
        compiler_params=_cparams("arbitrary"),
        name="moe_dispatch",
    )(end_p, dest_tiles, h.reshape(t // s, s, d))
    return xs


def _ffn_kernel(be_ref, nb_ref, xs_ref, wg_ref, wu_ref, bg_ref, bu_ref, wd_ref, bd_ref, y_ref):
    i = pl.program_id(0)

    @pl.when(i < nb_ref[0])
    def _():
        xb = xs_ref[...].astype(BF16)
        g = jnp.dot(xb, wg_ref[0], preferred_element_type=F32) + bg_ref[0]
        u = jnp.dot(xb, wu_ref[0], preferred_element_type=F32) + bu_ref[0]
        g = jnp.minimum(g, SWIGLU_LIMIT)
        u = jnp.clip(u, -SWIGLU_LIMIT, SWIGLU_LIMIT)
        a = (u + 1.0) * (g * jax.nn.sigmoid(SWIGLU_ALPHA * g))
        y_ref[...] = jnp.dot(a.astype(BF16), wd_ref[0, 0], preferred_element_type=F32) + bd_ref[0]

    @pl.when(i >= nb_ref[0])
    def _():
        y_ref[...] = jnp.zeros(y_ref.shape, F32)


def _expert_ffn(block_e, n_used, xs, w, rows):
    n_rows, d = xs.shape
    f = w["w_g"].shape[2]
    layer = w["layer"]
    ex = lambda i, be, nb: (be[i], 0, 0)
    return pl.pallas_call(
        _ffn_kernel,
        grid_spec=pltpu.PrefetchScalarGridSpec(
            num_scalar_prefetch=2,
            grid=(n_rows // rows,),
            in_specs=[
                pl.BlockSpec((rows, d), lambda i, be, nb: (jnp.minimum(i, nb[0] - 1), 0)),
                pl.BlockSpec((1, d, f), ex),
                pl.BlockSpec((1, d, f), ex),
                pl.BlockSpec((1, 1, f), ex),
                pl.BlockSpec((1, 1, f), ex),
                pl.BlockSpec((1, 1, f, d), lambda i, be, nb: (layer, be[i], 0, 0)),
                pl.BlockSpec((1, 1, d), ex),
            ],
            out_specs=pl.BlockSpec((rows, d), lambda i, be, nb: (i, 0)),
        ),
        out_shape=jax.ShapeDtypeStruct((n_rows, d), F32),
        compiler_params=_cparams("arbitrary"),
        name="moe_ffn",
    )(block_e, n_used, xs, w["w_g"], w["w_u"], w["b_g"], w["b_u"], w["w_d"], w["b_d"])


def _combine_kernel(dest_hbm, y_hbm, x_ref, gate_ref, mod_ref, fn_ref, o_ref, dsm, yg, isem, sem, *, tile, final):
    base = _fetch_rows_of_step(dest_hbm, dsm, isem, TOP_K * tile)

    def issue(grp, carry):
        for u in range(SUBLANES):
            for k in range(TOP_K):
                src = y_hbm.at[pl.ds(dsm[base + k * tile + grp * SUBLANES + u], 1), :]
                pltpu.make_async_copy(src, yg.at[k, grp, pl.ds(u, 1), :], sem).start(priority=k % 2)
        return carry

    lax.fori_loop(0, tile // SUBLANES, issue, 0)
    for k in range(TOP_K):
        pltpu.make_async_copy(yg.at[k], yg.at[k], sem).wait()
    gate = gate_ref[...]
    d = x_ref.shape[1]
    acc = yg[0].reshape(tile, d) * gate[:, 0:1]
    for k in range(1, TOP_K):
        acc = acc + yg[k].reshape(tile, d) * gate[:, k:k + 1]
    xn = x_ref[...] + _modrow(mod_ref, 5) * acc
    o_ref[...] = _rms(xn, fn_ref[...]) if final else xn


def _combine(dest_tiles, y, x, gate_t, mod, mod_row, final_norm, tile, final):
    t, d = x.shape
    s = SUBLANES
    return pl.pallas_call(
        functools.partial(_combine_kernel, tile=tile, final=final),
        grid=(t // tile,),
        in_specs=[
            pl.BlockSpec(memory_space=pl.ANY),
            pl.BlockSpec(memory_space=pl.ANY),
            pl.BlockSpec((tile, d), lambda i: (i, 0)),
            pl.BlockSpec((tile, TOP_K), lambda i: (i, 0)),
            pl.BlockSpec((1, 6, d), lambda i: (mod_row(i), 0, 0)),
            pl.BlockSpec((1, d), lambda i: (0, 0)),
        ],
        out_specs=pl.BlockSpec((tile, d), lambda i: (i, 0)),
        out_shape=jax.ShapeDtypeStruct((t, d), F32),
        scratch_shapes=[
            pltpu.SMEM((2 * TOP_K * tile,), jnp.int32),
            pltpu.VMEM((TOP_K, tile // s, s, d), F32),
            pltpu.SemaphoreType.DMA((2,)),
            pltpu.SemaphoreType.DMA,
        ],
        compiler_params=_cparams("arbitrary"),
        name="moe_combine",
    )(dest_tiles, y, x, gate_t, mod, final_norm)


def _moe(h, x, mod, mod_row_of_token, w, final_norm, final):
    t, d = h.shape
    rows = MOE_ROWS
    tile = min(COPY_TILE, t)
    assert t % tile == 0 and (TOP_K * tile) % 1024 == 0
    idx, gate, rank, counts = _route(h, w["w_router"], w["b_router"], ROUTER_TILE)
    padded = (counts + rows - 1) // rows * rows
    end_p = jnp.cumsum(padded)
    start_p = end_p - padded
    experts = jnp.arange(N_EXPERTS, dtype=jnp.int32)
    dest = jnp.sum(jnp.where(idx[:, :, None] == experts, start_p.astype(jnp.int32), 0), axis=-1) + rank
    n_rows = t * TOP_K + N_EXPERTS * rows
    n_blocks = n_rows // rows
    block_start = jnp.arange(n_blocks, dtype=jnp.int32) * rows
    block_e = jnp.minimum(jnp.sum(end_p[None, :] <= block_start[:, None], axis=1), N_EXPERTS - 1).astype(jnp.int32)
    n_used = (end_p[-1] // rows).astype(jnp.int32).reshape(1)
    dest_tiles = dest.reshape(TOP_K, t // tile, tile).transpose(1, 0, 2).reshape(-1)
    xs = _dispatch(end_p.astype(jnp.int32), dest_tiles, h, n_rows, tile, rows)
    y = _expert_ffn(block_e, n_used, xs, w, rows)
    return _combine(dest_tiles, y, x, gate.T, mod, lambda i: mod_row_of_token(i * tile), final_norm, tile, final)


def _log_sigmoid(x):
    return jnp.minimum(x, 0.0) - jnp.log1p(jnp.exp(-jnp.abs(x)))


def _gla_proj_kernel(x_ref, mod_ref, n1_ref, win_ref, wg1_ref, wg2_ref, bg_ref, q_ref, k_ref, v_ref, r_ref, g_ref,
                     *, nqk, nv, qscale):
    h = _rms(x_ref[...], n1_ref[...]) * (1.0 + _modrow(mod_ref, 1)) + _modrow(mod_ref, 0)
    hb = h.astype(BF16)
    a = jnp.dot(hb, win_ref[...], preferred_element_type=F32)
    q_ref[...] = a[:, :nqk] * qscale
    k_ref[...] = a[:, nqk:2 * nqk]
    v_ref[...] = a[:, 2 * nqk:2 * nqk + nv].astype(BF16)
    r_ref[...] = a[:, 2 * nqk + nv:]
    lo = jnp.dot(hb, wg1_ref[...], preferred_element_type=F32)
    lo_hi = lo.astype(BF16)
    lo_lo = (lo - lo_hi.astype(F32)).astype(BF16)
    z = jnp.dot(jnp.concatenate([lo_hi, lo_lo, lo_hi], axis=1), wg2_ref[...], preferred_element_type=F32) + bg_ref[...]
    for dr in range(2):
        g_ref[dr] = _log_sigmoid(z[:, dr * nqk:(dr + 1) * nqk]) * float(np.log2(np.e) / GLA_GATE_NORM)


def _gla_project(x, mod, mod_row_of_token, n1, w, tile):
    t, d = x.shape
    tile = min(tile, t)
    assert t % tile == 0
    nqk = w["w_gate2"].shape[1] // 2
    nv = (w["w_in"].shape[1] - 2 * nqk) // 2
    dk = nqk // GLA_HEADS
    const = lambda *shape: pl.BlockSpec(shape, lambda i: (0,) * len(shape))
    row = lambda n: pl.BlockSpec((tile, n), lambda i: (i, 0))
    return pl.pallas_call(
        functools.partial(_gla_proj_kernel, nqk=nqk, nv=nv, qscale=float(dk ** -0.5)),
        grid=(t // tile,),
        in_specs=[
            row(d),
            pl.BlockSpec((1, 6, d), lambda i: (mod_row_of_token(i * tile), 0, 0)),
            const(1, d),
            const(*w["w_in"].shape),
            const(*w["w_gate1"].shape),
            const(*w["w_gate2"].shape),
            const(*w["b_gate"].shape),
        ],
        out_specs=[row(nqk), row(nqk), row(nv), row(nv), pl.BlockSpec((2, tile, nqk), lambda i: (0, i, 0))],
        out_shape=[
            jax.ShapeDtypeStruct((t, nqk), F32),
            jax.ShapeDtypeStruct((t, nqk), F32),
            jax.ShapeDtypeStruct((t, nv), BF16),
            jax.ShapeDtypeStruct((t, nv), F32),
            jax.ShapeDtypeStruct((2, t, nqk), F32),
        ],
        compiler_params=_cparams("parallel"),
        name="gla_project",
    )(x, mod, n1, w["w_in"], w["w_gate1"], w["w_gate2"], w["b_gate"])


def _gla_level_tables(chunk):
    n_lvl = int(np.log2(chunk))
    i = np.arange(chunk)[:, None]
    j = np.arange(chunk)[None, :]
    level = np.full((chunk, chunk), -1, np.int32)
    for lv in range(n_lvl):
        s = chunk >> (lv + 1)
        same = (i // (2 * s)) == (j // (2 * s))
        level[same & ((i % (2 * s)) >= s) & ((j % (2 * s)) < s)] = lv
    level[i == j] = n_lvl
    lower = (j <= i)
    cum = np.stack([lower, lower.T]).astype(np.float32)
    return np.stack([level, level.T]), np.concatenate([cum, cum], axis=2)


def _ref_rows(b, s):
    c, n = b.shape
    if s >= 4:
        return jnp.broadcast_to(b.reshape(c // (2 * s), 2 * s, n)[:, s:s + 1, :], (c // (2 * s), 2 * s, n)).reshape(c, n)
    pos = lax.broadcasted_iota(jnp.int32, (c, n), 0) % (2 * s)
    out = b
    for p in range(2 * s):
        if p != s:
            out = jnp.where(pos == p, pltpu.roll(b, (p - s) % c, axis=0), out)
    return out


def _gla_scan_kernel(qf_ref, kf_ref, vf_ref, gf_ref, qb_ref, kb_ref, vb_ref, gb_ref, lvl_ref, cum_ref,
                     of_ref, ob_ref, *st_sc, chunk, dk, dv, nch):
    @pl.when(pl.program_id(1) == 0)
    def _():
        for st in st_sc:
            st[...] = jnp.zeros(st.shape, F32)

    n_lvl = int(np.log2(chunk))
    streams = ((qf_ref, kf_ref, vf_ref, gf_ref, of_ref), (qb_ref, kb_ref, vb_ref, gb_ref, ob_ref))

    lanes = [(dr, hh) for dr in range(2) for hh in range(GLA_HEADS)]
    cum = [cum_ref[dr].astype(BF16) for dr in range(2)]
    for ci in range(nch):
        qs, ks, vs, b, b_tot, rows = {}, {}, {}, {}, {}, {}
        for ln in lanes:
            dr, hh = ln
            q_ref, k_ref, v_ref, g_ref, _ = streams[dr]
            c_eff = ci if dr == 0 else nch - 1 - ci
            rows[ln] = slice(c_eff * chunk, (c_eff + 1) * chunk)
            qs[ln] = q_ref[rows[ln], hh * dk:(hh + 1) * dk]
            ks[ln] = k_ref[rows[ln], hh * dk:(hh + 1) * dk]
            vs[ln] = v_ref[rows[ln], hh * dv:(hh + 1) * dv]
            g = g_ref[0, rows[ln], hh * dk:(hh + 1) * dk]
            g_hi = g.astype(BF16)
            g_lo = (g - g_hi.astype(F32)).astype(BF16)
            b[ln] = jnp.dot(cum[dr], jnp.concatenate([g_hi, g_lo], axis=0), preferred_element_type=F32)
            b_tot[ln] = jnp.sum(g, axis=0, keepdims=True)
        qb = {ln: qs[ln].astype(BF16) for ln in lanes}
        kb = {ln: ks[ln].astype(BF16) for ln in lanes}
        a = {ln: jnp.where(lvl_ref[ln[0]] == n_lvl, lax.dot_general(qb[ln], kb[ln], _NT, preferred_element_type=F32), 0.0)
             for ln in lanes}
        for lv in range(n_lvl):
            for ln in lanes:
                e = jnp.exp2(-jnp.abs(b[ln] - _ref_rows(b[ln], chunk >> (lv + 1)))).astype(BF16)
                al = lax.dot_general(qb[ln] * e, kb[ln] * e, _NT, preferred_element_type=F32)
                a[ln] = jnp.where(lvl_ref[ln[0]] == lv, al, a[ln])
        for sidx, ln in enumerate(lanes):
            dr, hh = ln
            st = st_sc[sidx][...]
            q_in = (qs[ln] * jnp.exp2(b[ln])).astype(BF16)
            o = lax.dot_general(q_in, st.astype(BF16), _NT, preferred_element_type=F32)
            o = o + jnp.dot(a[ln].astype(BF16), vs[ln], preferred_element_type=F32)
            streams[dr][4][rows[ln], hh * dv:(hh + 1) * dv] = o
            k_out = (ks[ln] * jnp.exp2(b_tot[ln] - b[ln])).astype(BF16)
            st_sc[sidx][...] = st * jnp.exp2(b_tot[ln]) + lax.dot_general(vs[ln], k_out, _TN, preferred_element_type=F32)


def _gla_scan(q, k, v, g, batch, n_lat, n_ctx):
    t, nqk = q.shape
    nv = v.shape[1]
    blk = GLA_BLOCK
    assert n_ctx == blk and n_lat % blk == 0
    nlat = n_lat // blk
    ctx0 = batch * nlat
    chunk = GLA_CHUNK
    lvl, cum = _gla_level_tables(chunk)

    def rowblk(dr):
        def index(bb, j):
            lat = bb * nlat + (j - 1 if dr == 0 else nlat - j)
            return jnp.where(j == 0, ctx0 + bb, lat)
        return index

    def tok(n, dr):
        return pl.BlockSpec((blk, n), lambda bb, j: (rowblk(dr)(bb, j), 0))

    def gate(dr):
        return pl.BlockSpec((1, blk, nqk), lambda bb, j: (dr, rowblk(dr)(bb, j), 0))

    const = lambda *shape: pl.BlockSpec(shape, lambda bb, j: (0,) * len(shape))
    return pl.pallas_call(
        functools.partial(_gla_scan_kernel, chunk=chunk, dk=nqk // GLA_HEADS, dv=nv // GLA_HEADS, nch=blk // chunk),
        grid=(batch, nlat + 1),
        in_specs=[
            tok(nqk, 0), tok(nqk, 0), tok(nv, 0), gate(0),
            tok(nqk, 1), tok(nqk, 1), tok(nv, 1), gate(1),
            const(2, chunk, chunk),
            const(2, chunk, 2 * chunk),
        ],
        out_specs=[tok(nv, 0), tok(nv, 1)],
        out_shape=[jax.ShapeDtypeStruct((t, nv), F32)] * 2,
        scratch_shapes=[pltpu.VMEM((nv // GLA_HEADS, nqk // GLA_HEADS), F32)] * (2 * GLA_HEADS),
        compiler_params=_cparams("parallel", "arbitrary"),
        name="gla_scan",
    )(q, k, v, g, q, k, v, g, jnp.asarray(lvl), jnp.asarray(cum))


def _gla_finish_kernel(of_ref, ob_ref, r_ref, x_ref, mod_ref, hn_ref, wo_ref, n2_ref, xo_ref, h2_ref, *, dv):
    o = of_ref[...] + ob_ref[...]
    r = r_ref[...]
    parts = []
    for hh in range(GLA_HEADS):
        sl = slice(hh * dv, (hh + 1) * dv)
        parts.append(_rms(o[:, sl], hn_ref[...]) * (r[:, sl] * jax.nn.sigmoid(r[:, sl])))
    gated = jnp.concatenate(parts, axis=-1).astype(BF16)
    y = jnp.dot(gated, wo_ref[...], preferred_element_type=F32)
    xn = x_ref[...] + _modrow(mod_ref, 2) * y
    xo_ref[...] = xn
    h2_ref[...] = _rms(xn, n2_ref[...]) * (1.0 + _modrow(mod_ref, 4)) + _modrow(mod_ref, 3)


def _gla_finish(o_f, o_b, r, x, t, mod, mod_row_of_token, w, n2, tile):
    d = x.shape[1]
    nv = r.shape[1]
    tile = min(tile, t)
    assert t % tile == 0
    const = lambda *shape: pl.BlockSpec(shape, lambda i: (0,) * len(shape))
    row = lambda n: pl.BlockSpec((tile, n), lambda i: (i, 0))
    return pl.pallas_call(
        functools.partial(_gla_finish_kernel, dv=nv // GLA_HEADS),
        grid=(t // tile,),
        in_specs=[
            row(nv), row(nv), row(nv), row(d),
            pl.BlockSpec((1, 6, d), lambda i: (mod_row_of_token(i * tile), 0, 0)),
            const(1, nv // GLA_HEADS),
            const(nv, d),
            const(1, d),
        ],
        out_specs=[row(d), row(d)],
        out_shape=[jax.ShapeDtypeStruct((t, d), F32), jax.ShapeDtypeStruct((t, d), F32)],
        compiler_params=_cparams("parallel"),
        name="gla_finish",
    )(o_f, o_b, r, x, mod, w["head_norm"], w["w_o"], n2)


def _rope_tables(n):
    rows = n // GRID_W
    row = jnp.repeat(jnp.arange(rows, dtype=F32), GRID_W)
    col = jnp.tile(jnp.arange(GRID_W, dtype=F32), rows)
    nf = MLA_ROPE // 4
    inv = jnp.power(ROPE_BASE, -jnp.arange(nf, dtype=F32) / nf)
    ang = jnp.concatenate([row[:, None] * inv, col[:, None] * inv], axis=-1)
    cos, sin = jnp.cos(ang), jnp.sin(ang)
    cr, cc, sr, sc = cos[:, :nf], cos[:, nf:], sin[:, :nf], sin[:, nf:]
    cos64 = jnp.concatenate([cr, cr, cc, cc], axis=-1)
    sin64 = jnp.concatenate([-sr, sr, -sc, sc], axis=-1)
    return jnp.tile(cos64, (1, MLA_HEADS)), jnp.tile(sin64, (1, MLA_HEADS))


def _swap_halves(w):
    nf = MLA_ROPE // 4
    perm = np.concatenate([np.arange(nf, 2 * nf), np.arange(0, nf), np.arange(3 * nf, 4 * nf), np.arange(2 * nf, 3 * nf)])
    return w[..., perm]


def _mla_weights(w_in, q_norm, w_uq, kv_norm, w_ukv, w_o):
    hh = MLA_HEADS
    o = MLA_Q_RANK + MLA_KV_RANK
    w_in_ext = jnp.concatenate([w_in, _swap_halves(w_in[:, o:])], axis=1)
    uq = w_uq.reshape(MLA_Q_RANK, hh, MLA_NOPE + MLA_ROPE)
    w_uk = w_ukv[:, :hh * MLA_NOPE].reshape(MLA_KV_RANK, hh, MLA_NOPE)
    w_uv = w_ukv[:, hh * MLA_NOPE:].reshape(MLA_KV_RANK, hh, MLA_V)
    return {
        "w_in": w_in_ext.astype(BF16),
        "q_norm": q_norm.reshape(1, -1),
        "kv_norm": kv_norm.reshape(1, -1),
        "w_qn": uq[:, :, :MLA_NOPE].reshape(MLA_Q_RANK, hh * MLA_NOPE).astype(BF16),
        "w_qr": uq[:, :, MLA_NOPE:].reshape(MLA_Q_RANK, hh * MLA_ROPE).astype(BF16),
        "w_qs": _swap_halves(uq[:, :, MLA_NOPE:]).reshape(MLA_Q_RANK, hh * MLA_ROPE).astype(BF16),
        "w_ukT": jnp.transpose(w_uk, (1, 2, 0)).astype(BF16),
        "w_uv": jnp.transpose(w_uv, (1, 0, 2)).astype(BF16),
        "w_o": w_o.astype(BF16),
    }


def _split_gate2(w2):
    z = jnp.zeros_like(w2[0])
    bd = jnp.concatenate([jnp.concatenate([w2[0], z], axis=1), jnp.concatenate([z, w2[1]], axis=1)], axis=0)
    hi = bd.astype(BF16)
    lo = (bd - hi.astype(F32)).astype(BF16)
    return jnp.concatenate([hi, hi, lo], axis=0)


def _split_gu_kernel(w_ref, g_ref, u_ref, *, cols):
    r = lax.broadcasted_iota(jnp.int32, (cols, cols), 0)
    c = lax.broadcasted_iota(jnp.int32, (cols, cols), 1)
    src = jnp.where(c < cols // 2, 2 * c, 2 * (c - cols // 2) + 1)
    perm = jnp.where(r == src, 1.0, 0.0).astype(BF16)
    out = jnp.dot(w_ref[0, 0].astype(BF16), perm, preferred_element_type=F32).astype(BF16)
    g_ref[0] = out[:, :cols // 2]
    u_ref[0] = out[:, cols // 2:]


def _split_gate_up(w_gu, layer):
    _, e, d, f2 = w_gu.shape
    cols = 512
    half = pl.BlockSpec((1, d, cols // 2), lambda i, j: (i, 0, j))
    return pl.pallas_call(
        functools.partial(_split_gu_kernel, cols=cols),
        grid=(e, f2 // cols),
        in_specs=[pl.BlockSpec((1, 1, d, cols), lambda i, j: (layer, i, 0, j))],
        out_specs=[half, half],
        out_shape=[jax.ShapeDtypeStruct((e, d, f2 // 2), BF16)] * 2,
        compiler_params=_cparams("parallel", "parallel"),
        name="moe_split_gate_up",
    )(w_gu)


def _moe_weights(layer, w_router, b_router, w_gu, b_gu, w_down_bf16, b_down):
    e = w_gu.shape[1]
    w_g, w_u = _split_gate_up(w_gu, layer)
    return {
        "layer": layer,
        "w_router": w_router[layer],
        "b_router": b_router[layer],
        "w_g": w_g,
        "w_u": w_u,
        "b_g": b_gu[layer, :, 0::2].reshape(e, 1, -1),
        "b_u": b_gu[layer, :, 1::2].reshape(e, 1, -1),
        "w_d": w_down_bf16,
        "b_d": b_down[layer].reshape(e, 1, -1),
    }


def kernel(x, c, ctx, c_ctx, mod_w, mod_b, norm1, norm2, mla_w_in, mla_q_norm, mla_w_uq, mla_kv_norm, mla_w_ukv, mla_w_o, gla_w_in, gla_w_gate1, gla_w_gate2, gla_b_gate, gla_head_norm, gla_w_o, moe_w_router, moe_b_router, moe_w_gu, moe_b_gu, moe_w_down, moe_b_down, final_norm):
    B, N, D = x.shape
    Lc = ctx.shape[1]
    assert B < 8
    cc = jnp.zeros((8, D), F32).at[:B].set(c).at[B].set(c_ctx)
    mod = _modulation(cc, mod_w, mod_b).reshape(mod_w.shape[0], 8, 6, D)
    row_of_token = lambda tok: jnp.minimum(tok // N, B)
    n1 = norm1.reshape(-1, 1, D)
    n2 = norm2.reshape(-1, 1, D)
    fnorm = final_norm.reshape(1, D)

    mw = _mla_weights(mla_w_in[0], mla_q_norm[0], mla_w_uq[0], mla_kv_norm[0], mla_w_ukv[0], mla_w_o[0])
    cos, sin = _rope_tables(N)
    ones = jnp.ones((Lc, cos.shape[1]), F32)
    q_l, k_l = _mla_project(x, mod[0], lambda b: b, n1[0], mw, cos, sin, TOKEN_TILE)
    q_c, k_c = _mla_project(ctx, mod[0], lambda b: B, n1[0], mw, ones, jnp.zeros_like(ones), TOKEN_TILE)
    k_all = jnp.concatenate([k_l, k_c], axis=1)
    x1, h1 = _mla_attend(q_l, k_all, x, mod[0], lambda b: b, n2[0], mw, ATTN_Q_TILE, ATTN_KV_TILE)
    xc1, hc1 = _mla_attend(q_c, k_c, ctx, mod[0], lambda b: B, n2[0], mw, ATTN_Q_TILE, ATTN_KV_TILE)
    x_all = jnp.concatenate([x1, xc1], axis=0)
    h_all = jnp.concatenate([h1, hc1], axis=0)
    w_down_bf16 = moe_w_down.astype(BF16)
    ew = _moe_weights(0, moe_w_router, moe_b_router, moe_w_gu, moe_b_gu, w_down_bf16, moe_b_down)
    x_all = _moe(h_all, x_all, mod[0], row_of_token, ew, fnorm, final=False)

    gw = {
        "w_in": gla_w_in[0].astype(BF16),
        "w_gate1": jnp.concatenate([gla_w_gate1[0, 0], gla_w_gate1[0, 1]], axis=1).astype(BF16),
        "w_gate2": _split_gate2(gla_w_gate2[0]),
        "b_gate": gla_b_gate[0].reshape(1, -1),
        "head_norm": gla_head_norm[0].reshape(1, -1),
        "w_o": gla_w_o[0].astype(BF16),
    }
    q, k, v, r, g = _gla_project(x_all, mod[1], row_of_token, n1[1], gw, TOKEN_TILE)
    o_f, o_b = _gla_scan(q, k, v, g, B, N, Lc)
    x2, h2 = _gla_finish(o_f, o_b, r, x_all, B * N, mod[1], row_of_token, gw, n2[1], TOKEN_TILE)
    ew = _moe_weights(1, moe_w_router, moe_b_router, moe_w_gu, moe_b_gu, w_down_bf16, moe_b_down)
    out = _moe(h2, x2, mod[1], row_of_token, ew, fnorm, final=True)
    return out.reshape(B, N, D)
```

```python
import functools

import numpy as np
import jax
import jax.numpy as jnp
from jax import lax
from jax.experimental import pallas as pl
from jax.experimental.pallas import tpu as pltpu

F32 = jnp.float32
BF16 = jnp.bfloat16
EPS = 1e-6

GRID_W = 64
ROPE_BASE = 10000.0
MLA_HEADS = 8
MLA_Q_RANK = 512
MLA_KV_RANK = 256
MLA_NOPE = 128
MLA_ROPE = 64
MLA_V = 128
MLA_KDIM = MLA_KV_RANK + MLA_ROPE
GLA_HEADS = 4
GLA_GATE_RANK = 16
GLA_GATE_NORM = 16.0
GLA_CHUNK = 128
N_EXPERTS = 32
TOP_K = 4
SWIGLU_ALPHA = 1.702
SWIGLU_LIMIT = 7.0

VMEM_LIMIT_BYTES = 56 * 1024 * 1024
MOE_ROWS = 512
TOKEN_TILE = 512
ATTN_Q_TILE = 128
ATTN_KV_TILE = 768
ROUTER_TILE = 512
COPY_TILE = 512
GLA_BLOCK = 256

_HI = lax.Precision.HIGHEST
_NT = (((1,), (1,)), ((), ()))
_TN = (((0,), (0,)), ((), ()))


def _cparams(*sem):
    return pltpu.CompilerParams(dimension_semantics=sem, vmem_limit_bytes=VMEM_LIMIT_BYTES)


def _rms(x, g):
    return x * lax.rsqrt(jnp.mean(x * x, axis=-1, keepdims=True) + EPS) * g


def _modrow(mod_ref, k):
    return mod_ref[0, pl.ds(k, 1), :]


def _mod_kernel(cc_ref, w_ref, b_ref, o_ref):
    cc = cc_ref[...]
    s = cc * jax.nn.sigmoid(cc)
    o_ref[0] = jnp.dot(s, w_ref[0], preferred_element_type=F32, precision=_HI) + b_ref[0]


def _modulation(cc, mod_w, mod_b):
    depth, d, d6 = mod_w.shape
    tn = 1536
    return pl.pallas_call(
        _mod_kernel,
        grid=(depth, d6 // tn),
        in_specs=[
            pl.BlockSpec((8, d), lambda i, j: (0, 0)),
            pl.BlockSpec((1, d, tn), lambda i, j: (i, 0, j)),
            pl.BlockSpec((1, 1, tn), lambda i, j: (i, 0, j)),
        ],
        out_specs=pl.BlockSpec((1, 8, tn), lambda i, j: (i, 0, j)),
        out_shape=jax.ShapeDtypeStruct((depth, 8, d6), F32),
        compiler_params=_cparams("parallel", "parallel"),
        name="modulation",
    )(cc, mod_w, mod_b.reshape(depth, 1, d6))


def _mla_proj_kernel(x_ref, mod_ref, n1_ref, win_ref, qn_ref, kvn_ref, wqn_ref, wqr_ref, wqs_ref, wuk_ref,
                     cos_ref, sin_ref, q_ref, k_ref, *, scale):
    x = x_ref[0]
    h = _rms(x, n1_ref[...]) * (1.0 + _modrow(mod_ref, 1)) + _modrow(mod_ref, 0)
    a = jnp.dot(h.astype(BF16), win_ref[...], preferred_element_type=F32)
    cq = _rms(a[:, :MLA_Q_RANK], qn_ref[...]).astype(BF16)
    ckv = _rms(a[:, MLA_Q_RANK:MLA_Q_RANK + MLA_KV_RANK], kvn_ref[...])
    cos = cos_ref[...]
    sin = sin_ref[...]
    o = MLA_Q_RANK + MLA_KV_RANK
    kr = a[:, o:o + MLA_ROPE] * cos[:, :MLA_ROPE] + a[:, o + MLA_ROPE:o + 2 * MLA_ROPE] * sin[:, :MLA_ROPE]
    k_ref[0, :, :MLA_KV_RANK] = ckv.astype(BF16)
    k_ref[0, :, MLA_KV_RANK:] = kr.astype(BF16)
    qn = jnp.dot(cq, wqn_ref[...], preferred_element_type=F32)
    qr = (jnp.dot(cq, wqr_ref[...], preferred_element_type=F32) * cos
          + jnp.dot(cq, wqs_ref[...], preferred_element_type=F32) * sin)
    for hh in range(MLA_HEADS):
        qe = jnp.dot(qn[:, hh * MLA_NOPE:(hh + 1) * MLA_NOPE].astype(BF16), wuk_ref[hh], preferred_element_type=F32)
        q_ref[0, hh, :, :MLA_KV_RANK] = (qe * scale).astype(BF16)
        q_ref[0, hh, :, MLA_KV_RANK:] = (qr[:, hh * MLA_ROPE:(hh + 1) * MLA_ROPE] * scale).astype(BF16)


def _mla_project(x, mod, mod_row, n1, w, cos, sin, tile):
    b, l, d = x.shape
    tile = min(tile, l)
    assert l % tile == 0
    scale = float((MLA_NOPE + MLA_ROPE) ** -0.5 * np.log2(np.e))
    const = lambda *shape: pl.BlockSpec(shape, lambda i, j: (0,) * len(shape))
    return pl.pallas_call(
        functools.partial(_mla_proj_kernel, scale=scale),
        grid=(b, l // tile),
        in_specs=[
            pl.BlockSpec((1, tile, d), lambda i, j: (i, j, 0)),
            pl.BlockSpec((1, 6, d), lambda i, j: (mod_row(i), 0, 0)),
            const(1, d),
            const(*w["w_in"].shape),
            const(1, MLA_Q_RANK),
            const(1, MLA_KV_RANK),
            const(*w["w_qn"].shape),
            const(*w["w_qr"].shape),
            const(*w["w_qs"].shape),
            const(*w["w_ukT"].shape),
            pl.BlockSpec((tile, MLA_HEADS * MLA_ROPE), lambda i, j: (j, 0)),
            pl.BlockSpec((tile, MLA_HEADS * MLA_ROPE), lambda i, j: (j, 0)),
        ],
        out_specs=[
            pl.BlockSpec((1, MLA_HEADS, tile, MLA_KDIM), lambda i, j: (i, 0, j, 0)),
            pl.BlockSpec((1, tile, MLA_KDIM), lambda i, j: (i, j, 0)),
        ],
        out_shape=[
            jax.ShapeDtypeStruct((b, MLA_HEADS, l, MLA_KDIM), BF16),
            jax.ShapeDtypeStruct((b, l, MLA_KDIM), BF16),
        ],
        compiler_params=_cparams("parallel", "parallel"),
        name="mla_project",
    )(x, mod, n1, w["w_in"], w["q_norm"], w["kv_norm"], w["w_qn"], w["w_qr"], w["w_qs"], w["w_ukT"], cos, sin)


def _attn_kernel(q_ref, k_ref, x_ref, mod_ref, wuv_ref, wo_ref, n2_ref, xo_ref, h2_ref, m_sc, l_sc, acc_sc, *s_sc,
                 tq, tk, nkv):
    rows = MLA_HEADS * tq
    m_sc[...] = jnp.full(m_sc.shape, -jnp.inf, F32)
    l_sc[...] = jnp.zeros(l_sc.shape, F32)
    acc_sc[...] = jnp.zeros(acc_sc.shape, F32)

    def keys(c):
        return k_ref[0, pl.ds(pl.multiple_of(c * tk, tk), tk), :]

    def scores(c, slot):
        s_sc[slot][...] = lax.dot_general(q_ref[0].reshape(rows, MLA_KDIM), keys(c), _NT, preferred_element_type=F32)

    def absorb(c, slot):
        v = keys(c)[:, :MLA_KV_RANK]
        s = s_sc[slot][...]
        m_prev = m_sc[...]
        m_new = jnp.maximum(m_prev, jnp.max(s, axis=-1, keepdims=True))
        alpha = jnp.exp2(m_prev - m_new)
        p = jnp.exp2(s - m_new)
        l_sc[...] = alpha * l_sc[...] + jnp.sum(p, axis=-1, keepdims=True)
        acc_sc[...] = alpha * acc_sc[...] + jnp.dot(p.astype(BF16), v, preferred_element_type=F32)
        m_sc[...] = m_new

    def phase(c, slot, has_next):
        if has_next:
            scores(c + 1, 1 - slot)
        absorb(c, slot)

    scores(0, 0)
    npairs = (nkv - 1) // 2

    def body(j, carry):
        phase(2 * j, 0, True)
        phase(2 * j + 1, 1, True)
        return carry

    lax.fori_loop(0, npairs, body, 0)
    if nkv - 2 * npairs == 2:
        phase(nkv - 2, 0, True)
        phase(nkv - 1, 1, False)
    else:
        phase(nkv - 1, 0, False)
    u = []
    for hh in range(MLA_HEADS):
        o = (acc_sc[hh * tq:(hh + 1) * tq] / l_sc[hh * tq:(hh + 1) * tq]).astype(BF16)
        u.append(jnp.dot(o, wuv_ref[hh], preferred_element_type=F32))
    u = jnp.concatenate(u, axis=-1).astype(BF16)
    y = jnp.dot(u, wo_ref[...], preferred_element_type=F32)
    xn = x_ref[0] + _modrow(mod_ref, 2) * y
    xo_ref[...] = xn
    h2_ref[...] = _rms(xn, n2_ref[...]) * (1.0 + _modrow(mod_ref, 4)) + _modrow(mod_ref, 3)


def _mla_attend(q, keys, x, mod, mod_row, n2, w, tq, tk):
    b, l, d = x.shape
    lk = keys.shape[1]
    tq = min(tq, l)
    tk = min(tk, lk)
    assert lk % tk == 0 and l % tq == 0
    rows = MLA_HEADS * tq
    nq = l // tq
    const = lambda *shape: pl.BlockSpec(shape, lambda i, j: (0,) * len(shape))
    flat = pl.BlockSpec((tq, d), lambda i, j: (i * nq + j, 0))
    return pl.pallas_call(
        functools.partial(_attn_kernel, tq=tq, tk=tk, nkv=lk // tk),
        grid=(b, nq),
        in_specs=[
            pl.BlockSpec((1, MLA_HEADS, tq, MLA_KDIM), lambda i, j: (i, 0, j, 0)),
            pl.BlockSpec((1, lk, MLA_KDIM), lambda i, j: (i, 0, 0)),
            pl.BlockSpec((1, tq, d), lambda i, j: (i, j, 0)),
            pl.BlockSpec((1, 6, d), lambda i, j: (mod_row(i), 0, 0)),
            const(*w["w_uv"].shape),
            const(*w["w_o"].shape),
            const(1, d),
        ],
        out_specs=[flat, flat],
        out_shape=[jax.ShapeDtypeStruct((b * l, d), F32)] * 2,
        scratch_shapes=(
            [pltpu.VMEM((rows, 1), F32)] * 2 + [pltpu.VMEM((rows, MLA_KV_RANK), F32)] + [pltpu.VMEM((rows, tk), F32)] * 2
        ),
        compiler_params=_cparams("parallel", "arbitrary"),
        name="mla_attend",
    )(q, keys, x, mod, w["w_uv"], w["w_o"], n2)


def _router_kernel(h_ref, wt_ref, b_ref, idx_ref, gate_ref, rank_ref, cnt_ref, carry_sc, *, tile):
    @pl.when(pl.program_id(0) == 0)
    def _():
        carry_sc[...] = jnp.zeros(carry_sc.shape, F32)

    logits = lax.dot_general(wt_ref[...], h_ref[...], _NT, preferred_element_type=F32, precision=_HI) + b_ref[...]
    eid = lax.broadcasted_iota(jnp.int32, logits.shape, 0).astype(F32)
    vals, hots = [], []
    cur = logits
    for k in range(TOP_K):
        mx = jnp.max(cur, axis=0, keepdims=True)
        first = jnp.min(jnp.where(cur == mx, eid, float(N_EXPERTS)), axis=0, keepdims=True)
        hot = eid == first
        idx_ref[pl.ds(k, 1), :] = first.astype(jnp.int32)
        vals.append(mx)
        hots.append(hot)
        cur = jnp.where(hot, -jnp.inf, cur)
    ex = [jnp.exp(v - vals[0]) for v in vals]
    den = ex[0] + ex[1] + ex[2] + ex[3]
    for k in range(TOP_K):
        gate_ref[pl.ds(k, 1), :] = ex[k] / den
    cnt = sum(jnp.where(hot, 1.0, 0.0) for hot in hots)
    r = lax.broadcasted_iota(jnp.int32, (tile, tile), 0)
    c = lax.broadcasted_iota(jnp.int32, (tile, tile), 1)
    upper = jnp.where(r <= c, 1.0, 0.0).astype(BF16)
    incl = jnp.dot(cnt.astype(BF16), upper, preferred_element_type=F32)
    pos = carry_sc[...] + incl - 1.0
    for k in range(TOP_K):
        rank_ref[pl.ds(k, 1), :] = jnp.sum(jnp.where(hots[k], pos, 0.0), axis=0, keepdims=True).astype(jnp.int32)
    carry_sc[...] = carry_sc[...] + jnp.sum(cnt, axis=1, keepdims=True)
    cnt_ref[...] = jnp.broadcast_to(carry_sc[...], cnt_ref.shape)


def _route(h, w_router, b_router, tile):
    t, d = h.shape
    tile = min(tile, t)
    assert t % tile == 0
    tok = lambda i: (0, i)
    idx, gate, rank, cnt = pl.pallas_call(
        functools.partial(_router_kernel, tile=tile),
        grid=(t // tile,),
        in_specs=[
            pl.BlockSpec((tile, d), lambda i: (i, 0)),
            pl.BlockSpec((N_EXPERTS, d), lambda i: (0, 0)),
            pl.BlockSpec((N_EXPERTS, 1), lambda i: (0, 0)),
        ],
        out_specs=[
            pl.BlockSpec((TOP_K, tile), tok),
            pl.BlockSpec((TOP_K, tile), tok),
            pl.BlockSpec((TOP_K, tile), tok),
            pl.BlockSpec((N_EXPERTS, 128), lambda i: (0, 0)),
        ],
        out_shape=[
            jax.ShapeDtypeStruct((TOP_K, t), jnp.int32),
            jax.ShapeDtypeStruct((TOP_K, t), F32),
            jax.ShapeDtypeStruct((TOP_K, t), jnp.int32),
            jax.ShapeDtypeStruct((N_EXPERTS, 128), F32),
        ],
        scratch_shapes=[pltpu.VMEM((N_EXPERTS, 1), F32)],
        compiler_params=_cparams("arbitrary"),
        name="moe_router",
    )(h, w_router.T, b_router.reshape(N_EXPERTS, 1))
    return idx, gate, rank, cnt[:, 0].astype(jnp.int32)


SUBLANES = 8


def _fetch_rows_of_step(dest_hbm, dsm, isem, n):
    i = pl.program_id(0)
    slot = lax.rem(i, 2)

    def fetch(step, sl):
        return pltpu.make_async_copy(dest_hbm.at[pl.ds(pl.multiple_of(step * n, n), n)],
                                     dsm.at[pl.ds(pl.multiple_of(sl * n, n), n)], isem.at[sl])

    @pl.when(i == 0)
    def _():
        fetch(0, 0).start()

    @pl.when(i + 1 < pl.num_programs(0))
    def _():
        fetch(i + 1, 1 - slot).start()

    fetch(i, slot).wait()
    return slot * n


def _dispatch_kernel(endp_ref, dest_hbm, h_ref, xs_hbm, dsm, zbuf, isem, sem, zsem, *, tile, rows):
    i = pl.program_id(0)
    n = TOP_K * tile

    @pl.when(i == 0)
    def _():
        zbuf[...] = jnp.zeros(zbuf.shape, F32)

        def span(e):
            start = pl.multiple_of(jnp.maximum(endp_ref[e] - rows, 0), rows)
            return pltpu.make_async_copy(zbuf, xs_hbm.at[pl.ds(start, rows), :], zsem)

        for e in range(N_EXPERTS):
            span(e).start()
        for e in range(N_EXPERTS):
            span(e).wait()
        n_rows = xs_hbm.shape[0]
        for e in range(N_EXPERTS):
            start = pl.multiple_of(endp_ref[N_EXPERTS - 1] + e * rows, rows)

            @pl.when(start < n_rows)
            def _():
                tail = pltpu.make_async_copy(zbuf, xs_hbm.at[pl.ds(start, rows), :], zsem)
                tail.start()
                tail.wait()

    base = _fetch_rows_of_step(dest_hbm, dsm, isem, n)

    def issue(grp, carry):
        for u in range(SUBLANES):
            src = h_ref.at[grp, pl.ds(u, 1), :]
            for k in range(TOP_K):
                dst = xs_hbm.at[pl.ds(dsm[base + (grp * SUBLANES + u) * TOP_K + k], 1), :]
                pltpu.make_async_copy(src, dst, sem).start(priority=k % 2)
        return carry

    lax.fori_loop(0, tile // SUBLANES, issue, 0)
    for k in range(TOP_K):
        pltpu.make_async_copy(h_ref, h_ref, sem).wait()


def _dispatch(end_p, dest_tiles, h, n_rows, tile, rows):
    t, d = h.shape
    s = SUBLANES
    assert tile % s == 0 and rows % s == 0
    xs = pl.pallas_call(
        functools.partial(_dispatch_kernel, tile=tile, rows=rows),
        grid_spec=pltpu.PrefetchScalarGridSpec(
            num_scalar_prefetch=1,
            grid=(t // tile,),
            in_specs=[pl.BlockSpec(memory_space=pl.ANY), pl.BlockSpec((tile // s, s, d), lambda i, ep: (i, 0, 0))],
            out_specs=pl.BlockSpec(memory_space=pl.ANY),
            scratch_shapes=[
                pltpu.SMEM((2 * TOP_K * tile,), jnp.int32),
                pltpu.VMEM((rows, d), F32),
                pltpu.SemaphoreType.DMA((2,)),
                pltpu.SemaphoreType.DMA,
                pltpu.SemaphoreType.DMA,
            ],
        ),
        out_shape=jax.ShapeDtypeStruct((n_rows, d), F32),
        compiler_params=_cparams("arbitrary"),
        name="moe_dispatch",
    )(end_p, dest_tiles, h.reshape(t // s, s, d))
    return xs


FFN_SPLIT_COLS = 512


def _ffn_kernel(be_ref, nb_ref, xs_ref, wgu_ref, bg_ref, bu_ref, wd_ref, bd_ref, y_ref, wg_sc, wu_sc, wd_sc):
    i = pl.program_id(0)

    @pl.when(i < nb_ref[0])
    def _():
        @pl.when(jnp.logical_or(i == 0, be_ref[i] != be_ref[jnp.maximum(i - 1, 0)]))
        def _():
            cols = FFN_SPLIT_COLS
            r = lax.broadcasted_iota(jnp.int32, (cols, cols), 0)
            c = lax.broadcasted_iota(jnp.int32, (cols, cols), 1)
            src = jnp.where(c < cols // 2, 2 * c, 2 * (c - cols // 2) + 1)
            perm = jnp.where(r == src, 1.0, 0.0).astype(BF16)
            for j in range(wgu_ref.shape[3] // cols):
                blk = wgu_ref[0, 0, :, j * cols:(j + 1) * cols].astype(BF16)
                out = jnp.dot(blk, perm, preferred_element_type=F32).astype(BF16)
                wg_sc[:, j * (cols // 2):(j + 1) * (cols // 2)] = out[:, :cols // 2]
                wu_sc[:, j * (cols // 2):(j + 1) * (cols // 2)] = out[:, cols // 2:]
            wd_sc[...] = wd_ref[0, 0].astype(BF16)

        xb = xs_ref[...].astype(BF16)
        g = jnp.dot(xb, wg_sc[...], preferred_element_type=F32) + bg_ref[0]
        u = jnp.dot(xb, wu_sc[...], preferred_element_type=F32) + bu_ref[0]
        g = jnp.minimum(g, SWIGLU_LIMIT)
        u = jnp.clip(u, -SWIGLU_LIMIT, SWIGLU_LIMIT)
        a = (u + 1.0) * (g * jax.nn.sigmoid(SWIGLU_ALPHA * g))
        y_ref[...] = jnp.dot(a.astype(BF16), wd_sc[...], preferred_element_type=F32) + bd_ref[0]

    @pl.when(i >= nb_ref[0])
    def _():
        y_ref[...] = jnp.zeros(y_ref.shape, F32)


def _expert_ffn(block_e, n_used, xs, w, rows):
    n_rows, d = xs.shape
    f2 = w["w_gu"].shape[3]
    f = f2 // 2
    assert f2 % FFN_SPLIT_COLS == 0
    layer = w["layer"]
    ex = lambda i, be, nb: (be[i], 0, 0)
    lex = lambda i, be, nb: (layer, be[i], 0, 0)
    return pl.pallas_call(
        _ffn_kernel,
        grid_spec=pltpu.PrefetchScalarGridSpec(
            num_scalar_prefetch=2,
            grid=(n_rows // rows,),
            in_specs=[
                pl.BlockSpec((rows, d), lambda i, be, nb: (jnp.minimum(i, nb[0] - 1), 0)),
                pl.BlockSpec((1, 1, d, f2), lex),
                pl.BlockSpec((1, 1, f), ex),
                pl.BlockSpec((1, 1, f), ex),
                pl.BlockSpec((1, 1, f, d), lex),
                pl.BlockSpec((1, 1, d), ex),
            ],
            out_specs=pl.BlockSpec((rows, d), lambda i, be, nb: (i, 0)),
            scratch_shapes=[pltpu.VMEM((d, f), BF16), pltpu.VMEM((d, f), BF16), pltpu.VMEM((f, d), BF16)],
        ),
        out_shape=jax.ShapeDtypeStruct((n_rows, d), F32),
        compiler_params=_cparams("arbitrary"),
        name="moe_ffn",
    )(block_e, n_used, xs, w["w_gu"], w["b_g"], w["b_u"], w["w_d"], w["b_d"])


def _combine_kernel(dest_hbm, y_hbm, x_ref, gate_ref, mod_ref, fn_ref, o_ref, dsm, yg, isem, sem, *, tile, final):
    base = _fetch_rows_of_step(dest_hbm, dsm, isem, TOP_K * tile)

    def issue(grp, carry):
        for u in range(SUBLANES):
            for k in range(TOP_K):
                src = y_hbm.at[pl.ds(dsm[base + (grp * SUBLANES + u) * TOP_K + k], 1), :]
                pltpu.make_async_copy(src, yg.at[k, grp, pl.ds(u, 1), :], sem).start(priority=k % 2)
        return carry

    lax.fori_loop(0, tile // SUBLANES, issue, 0)
    for k in range(TOP_K):
        pltpu.make_async_copy(yg.at[k], yg.at[k], sem).wait()
    gate = gate_ref[...]
    d = x_ref.shape[1]
    acc = yg[0].reshape(tile, d) * gate[:, 0:1]
    for k in range(1, TOP_K):
        acc = acc + yg[k].reshape(tile, d) * gate[:, k:k + 1]
    xn = x_ref[...] + _modrow(mod_ref, 5) * acc
    o_ref[...] = _rms(xn, fn_ref[...]) if final else xn


def _combine(dest_tiles, y, x, gate_t, mod, mod_row, final_norm, tile, final):
    t, d = x.shape
    s = SUBLANES
    return pl.pallas_call(
        functools.partial(_combine_kernel, tile=tile, final=final),
        grid=(t // tile,),
        in_specs=[
            pl.BlockSpec(memory_space=pl.ANY),
            pl.BlockSpec(memory_space=pl.ANY),
            pl.BlockSpec((tile, d), lambda i: (i, 0)),
            pl.BlockSpec((tile, TOP_K), lambda i: (i, 0)),
            pl.BlockSpec((1, 6, d), lambda i: (mod_row(i), 0, 0)),
            pl.BlockSpec((1, d), lambda i: (0, 0)),
        ],
        out_specs=pl.BlockSpec((tile, d), lambda i: (i, 0)),
        out_shape=jax.ShapeDtypeStruct((t, d), F32),
        scratch_shapes=[
            pltpu.SMEM((2 * TOP_K * tile,), jnp.int32),
            pltpu.VMEM((TOP_K, tile // s, s, d), F32),
            pltpu.SemaphoreType.DMA((2,)),
            pltpu.SemaphoreType.DMA,
        ],
        compiler_params=_cparams("arbitrary"),
        name="moe_combine",
    )(dest_tiles, y, x, gate_t, mod, final_norm)


def _moe(h, x, mod, mod_row_of_token, w, final_norm, final):
    t, d = h.shape
    rows = MOE_ROWS
    tile = min(COPY_TILE, t)
    assert t % tile == 0 and (TOP_K * tile) % 1024 == 0
    idx, gate, rank, counts = _route(h, w["w_router"], w["b_router"], ROUTER_TILE)
    padded = (counts + rows - 1) // rows * rows
    end_p = jnp.cumsum(padded)
    start_p = end_p - padded
    experts = jnp.arange(N_EXPERTS, dtype=jnp.int32)
    dest = jnp.sum(jnp.where(idx[:, :, None] == experts, start_p.astype(jnp.int32), 0), axis=-1) + rank
    n_rows = t * TOP_K + N_EXPERTS * rows
    n_blocks = n_rows // rows
    block_start = jnp.arange(n_blocks, dtype=jnp.int32) * rows
    block_e = jnp.minimum(jnp.sum(end_p[None, :] <= block_start[:, None], axis=1), N_EXPERTS - 1).astype(jnp.int32)
    n_used = (end_p[-1] // rows).astype(jnp.int32).reshape(1)
    dest_tiles = dest.T.reshape(-1)
    xs = _dispatch(end_p.astype(jnp.int32), dest_tiles, h, n_rows, tile, rows)
    y = _expert_ffn(block_e, n_used, xs, w, rows)
    return _combine(dest_tiles, y, x, gate.T, mod, lambda i: mod_row_of_token(i * tile), final_norm, tile, final)


def _log_sigmoid(x):
    return jnp.minimum(x, 0.0) - jnp.log1p(jnp.exp(-jnp.abs(x)))


def _gla_proj_kernel(x_ref, mod_ref, n1_ref, win_ref, wg1_ref, wg2_ref, bg_ref, q_ref, k_ref, v_ref, r_ref, g_ref,
                     *, nqk, nv, qscale):
    h = _rms(x_ref[...], n1_ref[...]) * (1.0 + _modrow(mod_ref, 1)) + _modrow(mod_ref, 0)
    hb = h.astype(BF16)
    a = jnp.dot(hb, win_ref[...], preferred_element_type=F32)
    q_ref[...] = a[:, :nqk] * qscale
    k_ref[...] = a[:, nqk:2 * nqk]
    v_ref[...] = a[:, 2 * nqk:2 * nqk + nv].astype(BF16)
    r_ref[...] = a[:, 2 * nqk + nv:]
    lo = jnp.dot(hb, wg1_ref[...], preferred_element_type=F32)
    lo_hi = lo.astype(BF16)
    lo_lo = (lo - lo_hi.astype(F32)).astype(BF16)
    z = jnp.dot(jnp.concatenate([lo_hi, lo_lo, lo_hi], axis=1), wg2_ref[...], preferred_element_type=F32) + bg_ref[...]
    for dr in range(2):
        g_ref[dr] = _log_sigmoid(z[:, dr * nqk:(dr + 1) * nqk]) * float(np.log2(np.e) / GLA_GATE_NORM)


def _gla_project(x, mod, mod_row_of_token, n1, w, tile):
    t, d = x.shape
    tile = min(tile, t)
    assert t % tile == 0
    nqk = w["w_gate2"].shape[1] // 2
    nv = (w["w_in"].shape[1] - 2 * nqk) // 2
    dk = nqk // GLA_HEADS
    const = lambda *shape: pl.BlockSpec(shape, lambda i: (0,) * len(shape))
    row = lambda n: pl.BlockSpec((tile, n), lambda i: (i, 0))
    return pl.pallas_call(
        functools.partial(_gla_proj_kernel, nqk=nqk, nv=nv, qscale=float(dk ** -0.5)),
        grid=(t // tile,),
        in_specs=[
            row(d),
            pl.BlockSpec((1, 6, d), lambda i: (mod_row_of_token(i * tile), 0, 0)),
            const(1, d),
            const(*w["w_in"].shape),
            const(*w["w_gate1"].shape),
            const(*w["w_gate2"].shape),
            const(*w["b_gate"].shape),
        ],
        out_specs=[row(nqk), row(nqk), row(nv), row(nv), pl.BlockSpec((2, tile, nqk), lambda i: (0, i, 0))],
        out_shape=[
            jax.ShapeDtypeStruct((t, nqk), F32),
            jax.ShapeDtypeStruct((t, nqk), F32),
            jax.ShapeDtypeStruct((t, nv), BF16),
            jax.ShapeDtypeStruct((t, nv), F32),
            jax.ShapeDtypeStruct((2, t, nqk), F32),
        ],
        compiler_params=_cparams("parallel"),
        name="gla_project",
    )(x, mod, n1, w["w_in"], w["w_gate1"], w["w_gate2"], w["b_gate"])


def _gla_level_tables(chunk):
    n_lvl = int(np.log2(chunk))
    i = np.arange(chunk)[:, None]
    j = np.arange(chunk)[None, :]
    level = np.full((chunk, chunk), -1, np.int32)
    for lv in range(n_lvl):
        s = chunk >> (lv + 1)
        same = (i // (2 * s)) == (j // (2 * s))
        level[same & ((i % (2 * s)) >= s) & ((j % (2 * s)) < s)] = lv
    level[i == j] = n_lvl
    lower = (j <= i)
    cum = np.stack([lower, lower.T]).astype(np.float32)
    return np.stack([level, level.T]), np.concatenate([cum, cum], axis=2)


def _ref_rows(b, s):
    c, n = b.shape
    if s >= 4:
        return jnp.broadcast_to(b.reshape(c // (2 * s), 2 * s, n)[:, s:s + 1, :], (c // (2 * s), 2 * s, n)).reshape(c, n)
    pos = lax.broadcasted_iota(jnp.int32, (c, n), 0) % (2 * s)
    out = b
    for p in range(2 * s):
        if p != s:
            out = jnp.where(pos == p, pltpu.roll(b, (p - s) % c, axis=0), out)
    return out


def _gla_scan_kernel(qf_ref, kf_ref, vf_ref, gf_ref, qb_ref, kb_ref, vb_ref, gb_ref, lvl_ref, cum_ref,
                     of_ref, ob_ref, *st_sc, chunk, dk, dv, nch):
    @pl.when(pl.program_id(1) == 0)
    def _():
        for st in st_sc:
            st[...] = jnp.zeros(st.shape, F32)

    n_lvl = int(np.log2(chunk))
    streams = ((qf_ref, kf_ref, vf_ref, gf_ref, of_ref), (qb_ref, kb_ref, vb_ref, gb_ref, ob_ref))

    lanes = [(dr, hh) for dr in range(2) for hh in range(GLA_HEADS)]
    cum = [cum_ref[dr].astype(BF16) for dr in range(2)]
    for ci in range(nch):
        qs, ks, vs, b, b_tot, rows = {}, {}, {}, {}, {}, {}
        for ln in lanes:
            dr, hh = ln
            q_ref, k_ref, v_ref, g_ref, _ = streams[dr]
            c_eff = ci if dr == 0 else nch - 1 - ci
            rows[ln] = slice(c_eff * chunk, (c_eff + 1) * chunk)
            qs[ln] = q_ref[rows[ln], hh * dk:(hh + 1) * dk]
            ks[ln] = k_ref[rows[ln], hh * dk:(hh + 1) * dk]
            vs[ln] = v_ref[rows[ln], hh * dv:(hh + 1) * dv]
            g = g_ref[0, rows[ln], hh * dk:(hh + 1) * dk]
            g_hi = g.astype(BF16)
            g_lo = (g - g_hi.astype(F32)).astype(BF16)
            b[ln] = jnp.dot(cum[dr], jnp.concatenate([g_hi, g_lo], axis=0), preferred_element_type=F32)
            b_tot[ln] = jnp.sum(g, axis=0, keepdims=True)
        qb = {ln: qs[ln].astype(BF16) for ln in lanes}
        kb = {ln: ks[ln].astype(BF16) for ln in lanes}
        a = {ln: jnp.where(lvl_ref[ln[0]] == n_lvl, lax.dot_general(qb[ln], kb[ln], _NT, preferred_element_type=F32), 0.0)
             for ln in lanes}
        for lv in range(n_lvl):
            for ln in lanes:
                e = jnp.exp2(-jnp.abs(b[ln] - _ref_rows(b[ln], chunk >> (lv + 1)))).astype(BF16)
                al = lax.dot_general(qb[ln] * e, kb[ln] * e, _NT, preferred_element_type=F32)
                a[ln] = jnp.where(lvl_ref[ln[0]] == lv, al, a[ln])
        for sidx, ln in enumerate(lanes):
            dr, hh = ln
            st = st_sc[sidx][...]
            q_in = (qs[ln] * jnp.exp2(b[ln])).astype(BF16)
            o = lax.dot_general(q_in, st.astype(BF16), _NT, preferred_element_type=F32)
            o = o + jnp.dot(a[ln].astype(BF16), vs[ln], preferred_element_type=F32)
            streams[dr][4][rows[ln], hh * dv:(hh + 1) * dv] = o
            k_out = (ks[ln] * jnp.exp2(b_tot[ln] - b[ln])).astype(BF16)
            st_sc[sidx][...] = st * jnp.exp2(b_tot[ln]) + lax.dot_general(vs[ln], k_out, _TN, preferred_element_type=F32)


def _gla_scan(q, k, v, g, batch, n_lat, n_ctx):
    t, nqk = q.shape
    nv = v.shape[1]
    blk = GLA_BLOCK
    assert n_ctx == blk and n_lat % blk == 0
    nlat = n_lat // blk
    ctx0 = batch * nlat
    chunk = GLA_CHUNK
    lvl, cum = _gla_level_tables(chunk)

    def rowblk(dr):
        def index(bb, j):
            lat = bb * nlat + (j - 1 if dr == 0 else nlat - j)
            return jnp.where(j == 0, ctx0 + bb, lat)
        return index

    def tok(n, dr):
        return pl.BlockSpec((blk, n), lambda bb, j: (rowblk(dr)(bb, j), 0))

    def gate(dr):
        return pl.BlockSpec((1, blk, nqk), lambda bb, j: (dr, rowblk(dr)(bb, j), 0))

    const = lambda *shape: pl.BlockSpec(shape, lambda bb, j: (0,) * len(shape))
    return pl.pallas_call(
        functools.partial(_gla_scan_kernel, chunk=chunk, dk=nqk // GLA_HEADS, dv=nv // GLA_HEADS, nch=blk // chunk),
        grid=(batch, nlat + 1),
        in_specs=[
            tok(nqk, 0), tok(nqk, 0), tok(nv, 0), gate(0),
            tok(nqk, 1), tok(nqk, 1), tok(nv, 1), gate(1),
            const(2, chunk, chunk),
            const(2, chunk, 2 * chunk),
        ],
        out_specs=[tok(nv, 0), tok(nv, 1)],
        out_shape=[jax.ShapeDtypeStruct((t, nv), F32)] * 2,
        scratch_shapes=[pltpu.VMEM((nv // GLA_HEADS, nqk // GLA_HEADS), F32)] * (2 * GLA_HEADS),
        compiler_params=_cparams("parallel", "arbitrary"),
        name="gla_scan",
    )(q, k, v, g, q, k, v, g, jnp.asarray(lvl), jnp.asarray(cum))


def _gla_finish_kernel(of_ref, ob_ref, r_ref, x_ref, mod_ref, hn_ref, wo_ref, n2_ref, xo_ref, h2_ref, *, dv):
    o = of_ref[...] + ob_ref[...]
    r = r_ref[...]
    parts = []
    for hh in range(GLA_HEADS):
        sl = slice(hh * dv, (hh + 1) * dv)
        parts.append(_rms(o[:, sl], hn_ref[...]) * (r[:, sl] * jax.nn.sigmoid(r[:, sl])))
    gated = jnp.concatenate(parts, axis=-1).astype(BF16)
    y = jnp.dot(gated, wo_ref[...], preferred_element_type=F32)
    xn = x_ref[...] + _modrow(mod_ref, 2) * y
    xo_ref[...] = xn
    h2_ref[...] = _rms(xn, n2_ref[...]) * (1.0 + _modrow(mod_ref, 4)) + _modrow(mod_ref, 3)


def _gla_finish(o_f, o_b, r, x, t, mod, mod_row_of_token, w, n2, tile):
    d = x.shape[1]
    nv = r.shape[1]
    tile = min(tile, t)
    assert t % tile == 0
    const = lambda *shape: pl.BlockSpec(shape, lambda i: (0,) * len(shape))
    row = lambda n: pl.BlockSpec((tile, n), lambda i: (i, 0))
    return pl.pallas_call(
        functools.partial(_gla_finish_kernel, dv=nv // GLA_HEADS),
        grid=(t // tile,),
        in_specs=[
            row(nv), row(nv), row(nv), row(d),
            pl.BlockSpec((1, 6, d), lambda i: (mod_row_of_token(i * tile), 0, 0)),
            const(1, nv // GLA_HEADS),
            const(nv, d),
            const(1, d),
        ],
        out_specs=[row(d), row(d)],
        out_shape=[jax.ShapeDtypeStruct((t, d), F32), jax.ShapeDtypeStruct((t, d), F32)],
        compiler_params=_cparams("parallel"),
        name="gla_finish",
    )(o_f, o_b, r, x, mod, w["head_norm"], w["w_o"], n2)


def _rope_tables(n):
    rows = n // GRID_W
    row = jnp.repeat(jnp.arange(rows, dtype=F32), GRID_W)
    col = jnp.tile(jnp.arange(GRID_W, dtype=F32), rows)
    nf = MLA_ROPE // 4
    inv = jnp.power(ROPE_BASE, -jnp.arange(nf, dtype=F32) / nf)
    ang = jnp.concatenate([row[:, None] * inv, col[:, None] * inv], axis=-1)
    cos, sin = jnp.cos(ang), jnp.sin(ang)
    cr, cc, sr, sc = cos[:, :nf], cos[:, nf:], sin[:, :nf], sin[:, nf:]
    cos64 = jnp.concatenate([cr, cr, cc, cc], axis=-1)
    sin64 = jnp.concatenate([-sr, sr, -sc, sc], axis=-1)
    return jnp.tile(cos64, (1, MLA_HEADS)), jnp.tile(sin64, (1, MLA_HEADS))


def _swap_halves(w):
    nf = MLA_ROPE // 4
    perm = np.concatenate([np.arange(nf, 2 * nf), np.arange(0, nf), np.arange(3 * nf, 4 * nf), np.arange(2 * nf, 3 * nf)])
    return w[..., perm]


def _mla_weights(w_in, q_norm, w_uq, kv_norm, w_ukv, w_o):
    hh = MLA_HEADS
    o = MLA_Q_RANK + MLA_KV_RANK
    w_in_ext = jnp.concatenate([w_in, _swap_halves(w_in[:, o:])], axis=1)
    uq = w_uq.reshape(MLA_Q_RANK, hh, MLA_NOPE + MLA_ROPE)
    w_uk = w_ukv[:, :hh * MLA_NOPE].reshape(MLA_KV_RANK, hh, MLA_NOPE)
    w_uv = w_ukv[:, hh * MLA_NOPE:].reshape(MLA_KV_RANK, hh, MLA_V)
    return {
        "w_in": w_in_ext.astype(BF16),
        "q_norm": q_norm.reshape(1, -1),
        "kv_norm": kv_norm.reshape(1, -1),
        "w_qn": uq[:, :, :MLA_NOPE].reshape(MLA_Q_RANK, hh * MLA_NOPE).astype(BF16),
        "w_qr": uq[:, :, MLA_NOPE:].reshape(MLA_Q_RANK, hh * MLA_ROPE).astype(BF16),
        "w_qs": _swap_halves(uq[:, :, MLA_NOPE:]).reshape(MLA_Q_RANK, hh * MLA_ROPE).astype(BF16),
        "w_ukT": jnp.transpose(w_uk, (1, 2, 0)).astype(BF16),
        "w_uv": jnp.transpose(w_uv, (1, 0, 2)).astype(BF16),
        "w_o": w_o.astype(BF16),
    }


def _split_gate2(w2):
    z = jnp.zeros_like(w2[0])
    bd = jnp.concatenate([jnp.concatenate([w2[0], z], axis=1), jnp.concatenate([z, w2[1]], axis=1)], axis=0)
    hi = bd.astype(BF16)
    lo = (bd - hi.astype(F32)).astype(BF16)
    return jnp.concatenate([hi, hi, lo], axis=0)


def _moe_weights(layer, w_router, b_router, w_gu, b_gu, w_down, b_down):
    e = w_gu.shape[1]
    return {
        "layer": layer,
        "w_router": w_router[layer],
        "b_router": b_router[layer],
        "w_gu": w_gu,
        "b_g": b_gu[layer, :, 0::2].reshape(e, 1, -1),
        "b_u": b_gu[layer, :, 1::2].reshape(e, 1, -1),
        "w_d": w_down,
        "b_d": b_down[layer].reshape(e, 1, -1),
    }


def kernel(x, c, ctx, c_ctx, mod_w, mod_b, norm1, norm2, mla_w_in, mla_q_norm, mla_w_uq, mla_kv_norm, mla_w_ukv, mla_w_o, gla_w_in, gla_w_gate1, gla_w_gate2, gla_b_gate, gla_head_norm, gla_w_o, moe_w_router, moe_b_router, moe_w_gu, moe_b_gu, moe_w_down, moe_b_down, final_norm):
    B, N, D = x.shape
    Lc = ctx.shape[1]
    assert B < 8
    cc = jnp.zeros((8, D), F32).at[:B].set(c).at[B].set(c_ctx)
    mod = _modulation(cc, mod_w, mod_b).reshape(mod_w.shape[0], 8, 6, D)
    row_of_token = lambda tok: jnp.minimum(tok // N, B)
    n1 = norm1.reshape(-1, 1, D)
    n2 = norm2.reshape(-1, 1, D)
    fnorm = final_norm.reshape(1, D)

    mw = _mla_weights(mla_w_in[0], mla_q_norm[0], mla_w_uq[0], mla_kv_norm[0], mla_w_ukv[0], mla_w_o[0])
    cos, sin = _rope_tables(N)
    ones = jnp.ones((Lc, cos.shape[1]), F32)
    q_l, k_l = _mla_project(x, mod[0], lambda b: b, n1[0], mw, cos, sin, TOKEN_TILE)
    q_c, k_c = _mla_project(ctx, mod[0], lambda b: B, n1[0], mw, ones, jnp.zeros_like(ones), TOKEN_TILE)
    k_all = jnp.concatenate([k_l, k_c], axis=1)
    x1, h1 = _mla_attend(q_l, k_all, x, mod[0], lambda b: b, n2[0], mw, ATTN_Q_TILE, ATTN_KV_TILE)
    xc1, hc1 = _mla_attend(q_c, k_c, ctx, mod[0], lambda b: B, n2[0], mw, ATTN_Q_TILE, ATTN_KV_TILE)
    x_all = jnp.concatenate([x1, xc1], axis=0)
    h_all = jnp.concatenate([h1, hc1], axis=0)
    ew = _moe_weights(0, moe_w_router, moe_b_router, moe_w_gu, moe_b_gu, moe_w_down, moe_b_down)
    x_all = _moe(h_all, x_all, mod[0], row_of_token, ew, fnorm, final=False)

    gw = {
        "w_in": gla_w_in[0].astype(BF16),
        "w_gate1": jnp.concatenate([gla_w_gate1[0, 0], gla_w_gate1[0, 1]], axis=1).astype(BF16),
        "w_gate2": _split_gate2(gla_w_gate2[0]),
        "b_gate": gla_b_gate[0].reshape(1, -1),
        "head_norm": gla_head_norm[0].reshape(1, -1),
        "w_o": gla_w_o[0].astype(BF16),
    }
    q, k, v, r, g = _gla_project(x_all, mod[1], row_of_token, n1[1], gw, TOKEN_TILE)
    o_f, o_b = _gla_scan(q, k, v, g, B, N, Lc)
    x2, h2 = _gla_finish(o_f, o_b, r, x_all, B * N, mod[1], row_of_token, gw, n2[1], TOKEN_TILE)
    ew = _moe_weights(1, moe_w_router, moe_b_router, moe_w_gu, moe_b_gu, moe_w_down, moe_b_down)
    out = _moe(h2, x2, mod[1], row_of_token, ew, fnorm, final=True)
    return out.reshape(B, N, D)
```

```python
import functools

import numpy as np
import jax
import jax.numpy as jnp
from jax import lax
from jax.experimental import pallas as pl
from jax.experimental.pallas import tpu as pltpu

F32 = jnp.float32
BF16 = jnp.bfloat16
EPS = 1e-6

GRID_W = 64
ROPE_BASE = 10000.0
MLA_HEADS = 8
MLA_Q_RANK = 512
MLA_KV_RANK = 256
MLA_NOPE = 128
MLA_ROPE = 64
MLA_V = 128
MLA_KDIM = MLA_KV_RANK + MLA_ROPE
GLA_HEADS = 4
GLA_GATE_RANK = 16
GLA_GATE_NORM = 16.0
GLA_CHUNK = 128
N_EXPERTS = 32
TOP_K = 4
SWIGLU_ALPHA = 1.702
SWIGLU_LIMIT = 7.0

VMEM_LIMIT_BYTES = 56 * 1024 * 1024
MOE_ROWS = 512
TOKEN_TILE = 512
ATTN_Q_TILE = 128
ATTN_KV_TILE = 768
ROUTER_TILE = 512
COPY_TILE = 512
GLA_BLOCK = 256

_HI = lax.Precision.HIGHEST
_NT = (((1,), (1,)), ((), ()))
_TN = (((0,), (0,)), ((), ()))


def _cparams(*sem):
    return pltpu.CompilerParams(dimension_semantics=sem, vmem_limit_bytes=VMEM_LIMIT_BYTES)


def _rms(x, g):
    return x * lax.rsqrt(jnp.mean(x * x, axis=-1, keepdims=True) + EPS) * g


def _modrow(mod_ref, k):
    return mod_ref[0, pl.ds(k, 1), :]


def _mod_kernel(cc_ref, w_ref, b_ref, o_ref):
    cc = cc_ref[...]
    s = cc * jax.nn.sigmoid(cc)
    o_ref[0] = jnp.dot(s, w_ref[0], preferred_element_type=F32, precision=_HI) + b_ref[0]


def _modulation(cc, mod_w, mod_b):
    depth, d, d6 = mod_w.shape
    tn = 1536
    return pl.pallas_call(
        _mod_kernel,
        grid=(depth, d6 // tn),
        in_specs=[
            pl.BlockSpec((8, d), lambda i, j: (0, 0)),
            pl.BlockSpec((1, d, tn), lambda i, j: (i, 0, j)),
            pl.BlockSpec((1, 1, tn), lambda i, j: (i, 0, j)),
        ],
        out_specs=pl.BlockSpec((1, 8, tn), lambda i, j: (i, 0, j)),
        out_shape=jax.ShapeDtypeStruct((depth, 8, d6), F32),
        compiler_params=_cparams("parallel", "parallel"),
        name="modulation",
    )(cc, mod_w, mod_b.reshape(depth, 1, d6))


def _mla_proj_kernel(x_ref, mod_ref, n1_ref, win_ref, qn_ref, kvn_ref, wqn_ref, wqr_ref, wqs_ref, wuk_ref,
                     cos_ref, sin_ref, q_ref, k_ref, *, scale):
    x = x_ref[0]
    h = _rms(x, n1_ref[...]) * (1.0 + _modrow(mod_ref, 1)) + _modrow(mod_ref, 0)
    a = jnp.dot(h.astype(BF16), win_ref[...], preferred_element_type=F32)
    cq = _rms(a[:, :MLA_Q_RANK], qn_ref[...]).astype(BF16)
    ckv = _rms(a[:, MLA_Q_RANK:MLA_Q_RANK + MLA_KV_RANK], kvn_ref[...])
    cos = cos_ref[...]
    sin = sin_ref[...]
    o = MLA_Q_RANK + MLA_KV_RANK
    kr = a[:, o:o + MLA_ROPE] * cos[:, :MLA_ROPE] + a[:, o + MLA_ROPE:o + 2 * MLA_ROPE] * sin[:, :MLA_ROPE]
    k_ref[0, :, :MLA_KV_RANK] = ckv.astype(BF16)
    k_ref[0, :, MLA_KV_RANK:] = kr.astype(BF16)
    qn = jnp.dot(cq, wqn_ref[...], preferred_element_type=F32)
    qr = (jnp.dot(cq, wqr_ref[...], preferred_element_type=F32) * cos
          + jnp.dot(cq, wqs_ref[...], preferred_element_type=F32) * sin)
    for hh in range(MLA_HEADS):
        qe = jnp.dot(qn[:, hh * MLA_NOPE:(hh + 1) * MLA_NOPE].astype(BF16), wuk_ref[hh], preferred_element_type=F32)
        q_ref[0, hh, :, :MLA_KV_RANK] = (qe * scale).astype(BF16)
        q_ref[0, hh, :, MLA_KV_RANK:] = (qr[:, hh * MLA_ROPE:(hh + 1) * MLA_ROPE] * scale).astype(BF16)


def _mla_project(x, mod, mod_row, n1, w, cos, sin, tile):
    b, l, d = x.shape
    tile = min(tile, l)
    assert l % tile == 0
    scale = float((MLA_NOPE + MLA_ROPE) ** -0.5 * np.log2(np.e))
    const = lambda *shape: pl.BlockSpec(shape, lambda i, j: (0,) * len(shape))
    return pl.pallas_call(
        functools.partial(_mla_proj_kernel, scale=scale),
        grid=(b, l // tile),
        in_specs=[
            pl.BlockSpec((1, tile, d), lambda i, j: (i, j, 0)),
            pl.BlockSpec((1, 6, d), lambda i, j: (mod_row(i), 0, 0)),
            const(1, d),
            const(*w["w_in"].shape),
            const(1, MLA_Q_RANK),
            const(1, MLA_KV_RANK),
            const(*w["w_qn"].shape),
            const(*w["w_qr"].shape),
            const(*w["w_qs"].shape),
            const(*w["w_ukT"].shape),
            pl.BlockSpec((tile, MLA_HEADS * MLA_ROPE), lambda i, j: (j, 0)),
            pl.BlockSpec((tile, MLA_HEADS * MLA_ROPE), lambda i, j: (j, 0)),
        ],
        out_specs=[
            pl.BlockSpec((1, MLA_HEADS, tile, MLA_KDIM), lambda i, j: (i, 0, j, 0)),
            pl.BlockSpec((1, tile, MLA_KDIM), lambda i, j: (i, j, 0)),
        ],
        out_shape=[
            jax.ShapeDtypeStruct((b, MLA_HEADS, l, MLA_KDIM), BF16),
            jax.ShapeDtypeStruct((b, l, MLA_KDIM), BF16),
        ],
        compiler_params=_cparams("parallel", "parallel"),
        name="mla_project",
    )(x, mod, n1, w["w_in"], w["q_norm"], w["kv_norm"], w["w_qn"], w["w_qr"], w["w_qs"], w["w_ukT"], cos, sin)


def _attn_kernel(q_ref, k_ref, x_ref, mod_ref, wuv_ref, wo_ref, n2_ref, xo_ref, h2_ref, m_sc, l_sc, acc_sc, *s_sc,
                 tq, tk, nkv):
    rows = MLA_HEADS * tq
    m_sc[...] = jnp.full(m_sc.shape, -jnp.inf, F32)
    l_sc[...] = jnp.zeros(l_sc.shape, F32)
    acc_sc[...] = jnp.zeros(acc_sc.shape, F32)

    def keys(c):
        return k_ref[0, pl.ds(pl.multiple_of(c * tk, tk), tk), :]

    def scores(c, slot):
        s_sc[slot][...] = lax.dot_general(q_ref[0].reshape(rows, MLA_KDIM), keys(c), _NT, preferred_element_type=F32)

    def absorb(c, slot):
        v = keys(c)[:, :MLA_KV_RANK]
        s = s_sc[slot][...]
        m_prev = m_sc[...]
        m_new = jnp.maximum(m_prev, jnp.max(s, axis=-1, keepdims=True))
        alpha = jnp.exp2(m_prev - m_new)
        p = jnp.exp2(s - m_new)
        l_sc[...] = alpha * l_sc[...] + jnp.sum(p, axis=-1, keepdims=True)
        acc_sc[...] = alpha * acc_sc[...] + jnp.dot(p.astype(BF16), v, preferred_element_type=F32)
        m_sc[...] = m_new

    def phase(c, slot, has_next):
        if has_next:
            scores(c + 1, 1 - slot)
        absorb(c, slot)

    scores(0, 0)
    npairs = (nkv - 1) // 2

    def body(j, carry):
        phase(2 * j, 0, True)
        phase(2 * j + 1, 1, True)
        return carry

    lax.fori_loop(0, npairs, body, 0)
    if nkv - 2 * npairs == 2:
        phase(nkv - 2, 0, True)
        phase(nkv - 1, 1, False)
    else:
        phase(nkv - 1, 0, False)
    u = []
    for hh in range(MLA_HEADS):
        o = (acc_sc[hh * tq:(hh + 1) * tq] / l_sc[hh * tq:(hh + 1) * tq]).astype(BF16)
        u.append(jnp.dot(o, wuv_ref[hh], preferred_element_type=F32))
    u = jnp.concatenate(u, axis=-1).astype(BF16)
    y = jnp.dot(u, wo_ref[...], preferred_element_type=F32)
    xn = x_ref[0] + _modrow(mod_ref, 2) * y
    xo_ref[...] = xn
    h2_ref[...] = _rms(xn, n2_ref[...]) * (1.0 + _modrow(mod_ref, 4)) + _modrow(mod_ref, 3)


def _mla_attend(q, keys, x, mod, mod_row, n2, w, tq, tk):
    b, l, d = x.shape
    lk = keys.shape[1]
    tq = min(tq, l)
    tk = min(tk, lk)
    assert lk % tk == 0 and l % tq == 0
    rows = MLA_HEADS * tq
    nq = l // tq
    const = lambda *shape: pl.BlockSpec(shape, lambda i, j: (0,) * len(shape))
    flat = pl.BlockSpec((tq, d), lambda i, j: (i * nq + j, 0))
    return pl.pallas_call(
        functools.partial(_attn_kernel, tq=tq, tk=tk, nkv=lk // tk),
        grid=(b, nq),
        in_specs=[
            pl.BlockSpec((1, MLA_HEADS, tq, MLA_KDIM), lambda i, j: (i, 0, j, 0)),
            pl.BlockSpec((1, lk, MLA_KDIM), lambda i, j: (i, 0, 0)),
            pl.BlockSpec((1, tq, d), lambda i, j: (i, j, 0)),
            pl.BlockSpec((1, 6, d), lambda i, j: (mod_row(i), 0, 0)),
            const(*w["w_uv"].shape),
            const(*w["w_o"].shape),
            const(1, d),
        ],
        out_specs=[flat, flat],
        out_shape=[jax.ShapeDtypeStruct((b * l, d), F32)] * 2,
        scratch_shapes=(
            [pltpu.VMEM((rows, 1), F32)] * 2 + [pltpu.VMEM((rows, MLA_KV_RANK), F32)] + [pltpu.VMEM((rows, tk), F32)] * 2
        ),
        compiler_params=_cparams("parallel", "arbitrary"),
        name="mla_attend",
    )(q, keys, x, mod, w["w_uv"], w["w_o"], n2)


def _router_kernel(h_ref, wt_ref, b_ref, idx_ref, gate_ref, rank_ref, cnt_ref, carry_sc, *, tile):
    @pl.when(pl.program_id(0) == 0)
    def _():
        carry_sc[...] = jnp.zeros(carry_sc.shape, F32)

    logits = lax.dot_general(wt_ref[...], h_ref[...], _NT, preferred_element_type=F32, precision=_HI) + b_ref[...]
    eid = lax.broadcasted_iota(jnp.int32, logits.shape, 0).astype(F32)
    vals, hots = [], []
    cur = logits
    for k in range(TOP_K):
        mx = jnp.max(cur, axis=0, keepdims=True)
        first = jnp.min(jnp.where(cur == mx, eid, float(N_EXPERTS)), axis=0, keepdims=True)
        hot = eid == first
        idx_ref[pl.ds(k, 1), :] = first.astype(jnp.int32)
        vals.append(mx)
        hots.append(hot)
        cur = jnp.where(hot, -jnp.inf, cur)
    ex = [jnp.exp(v - vals[0]) for v in vals]
    den = ex[0] + ex[1] + ex[2] + ex[3]
    for k in range(TOP_K):
        gate_ref[pl.ds(k, 1), :] = ex[k] / den
    cnt = sum(jnp.where(hot, 1.0, 0.0) for hot in hots)
    r = lax.broadcasted_iota(jnp.int32, (tile, tile), 0)
    c = lax.broadcasted_iota(jnp.int32, (tile, tile), 1)
    upper = jnp.where(r <= c, 1.0, 0.0).astype(BF16)
    incl = jnp.dot(cnt.astype(BF16), upper, preferred_element_type=F32)
    pos = carry_sc[...] + incl - 1.0
    for k in range(TOP_K):
        rank_ref[pl.ds(k, 1), :] = jnp.sum(jnp.where(hots[k], pos, 0.0), axis=0, keepdims=True).astype(jnp.int32)
    carry_sc[...] = carry_sc[...] + jnp.sum(cnt, axis=1, keepdims=True)
    cnt_ref[...] = jnp.broadcast_to(carry_sc[...], cnt_ref.shape)


def _route(h, w_router, b_router, tile):
    t, d = h.shape
    tile = min(tile, t)
    assert t % tile == 0
    tok = lambda i: (0, i)
    idx, gate, rank, cnt = pl.pallas_call(
        functools.partial(_router_kernel, tile=tile),
        grid=(t // tile,),
        in_specs=[
            pl.BlockSpec((tile, d), lambda i: (i, 0)),
            pl.BlockSpec((N_EXPERTS, d), lambda i: (0, 0)),
            pl.BlockSpec((N_EXPERTS, 1), lambda i: (0, 0)),
        ],
        out_specs=[
            pl.BlockSpec((TOP_K, tile), tok),
            pl.BlockSpec((TOP_K, tile), tok),
            pl.BlockSpec((TOP_K, tile), tok),
            pl.BlockSpec((N_EXPERTS, 128), lambda i: (0, 0)),
        ],
        out_shape=[
            jax.ShapeDtypeStruct((TOP_K, t), jnp.int32),
            jax.ShapeDtypeStruct((TOP_K, t), F32),
            jax.ShapeDtypeStruct((TOP_K, t), jnp.int32),
            jax.ShapeDtypeStruct((N_EXPERTS, 128), F32),
        ],
        scratch_shapes=[pltpu.VMEM((N_EXPERTS, 1), F32)],
        compiler_params=_cparams("arbitrary"),
        name="moe_router",
    )(h, w_router.T, b_router.reshape(N_EXPERTS, 1))
    return idx, gate, rank, cnt[:, 0].astype(jnp.int32)


SUBLANES = 8


def _fetch_rows_of_step(dest_hbm, dsm, isem, n):
    i = pl.program_id(0)
    slot = lax.rem(i, 2)

    def fetch(step, sl):
        return pltpu.make_async_copy(dest_hbm.at[pl.ds(pl.multiple_of(step * n, n), n)],
                                     dsm.at[pl.ds(pl.multiple_of(sl * n, n), n)], isem.at[sl])

    @pl.when(i == 0)
    def _():
        fetch(0, 0).start()

    @pl.when(i + 1 < pl.num_programs(0))
    def _():
        fetch(i + 1, 1 - slot).start()

    fetch(i, slot).wait()
    return slot * n


def _dispatch_kernel(endp_ref, dest_hbm, h_ref, xs_hbm, dsm, zbuf, isem, sem, zsem, *, tile, rows):
    i = pl.program_id(0)
    n = TOP_K * tile

    @pl.when(i == 0)
    def _():
        zbuf[...] = jnp.zeros(zbuf.shape, F32)

        def span(e):
            start = pl.multiple_of(jnp.maximum(endp_ref[e] - rows, 0), rows)
            return pltpu.make_async_copy(zbuf, xs_hbm.at[pl.ds(start, rows), :], zsem)

        for e in range(N_EXPERTS):
            span(e).start()
        for e in range(N_EXPERTS):
            span(e).wait()
        n_rows = xs_hbm.shape[0]
        for e in range(N_EXPERTS):
            start = pl.multiple_of(endp_ref[N_EXPERTS - 1] + e * rows, rows)

            @pl.when(start < n_rows)
            def _():
                tail = pltpu.make_async_copy(zbuf, xs_hbm.at[pl.ds(start, rows), :], zsem)
                tail.start()
                tail.wait()

    base = _fetch_rows_of_step(dest_hbm, dsm, isem, n)

    def issue(grp, carry):
        for u in range(SUBLANES):
            src = h_ref.at[grp, pl.ds(u, 1), :]
            for k in range(TOP_K):
                dst = xs_hbm.at[pl.ds(dsm[base + (grp * SUBLANES + u) * TOP_K + k], 1), :]
                pltpu.make_async_copy(src, dst, sem).start(priority=k % 2)
        return carry

    lax.fori_loop(0, tile // SUBLANES, issue, 0)
    for k in range(TOP_K):
        pltpu.make_async_copy(h_ref, h_ref, sem).wait()


def _dispatch(end_p, dest_tiles, h, n_rows, tile, rows):
    t, d = h.shape
    s = SUBLANES
    assert tile % s == 0 and rows % s == 0
    xs = pl.pallas_call(
        functools.partial(_dispatch_kernel, tile=tile, rows=rows),
        grid_spec=pltpu.PrefetchScalarGridSpec(
            num_scalar_prefetch=1,
            grid=(t // tile,),
            in_specs=[pl.BlockSpec(memory_space=pl.ANY), pl.BlockSpec((tile // s, s, d), lambda i, ep: (i, 0, 0))],
            out_specs=pl.BlockSpec(memory_space=pl.ANY),
            scratch_shapes=[
                pltpu.SMEM((2 * TOP_K * tile,), jnp.int32),
                pltpu.VMEM((rows, d), F32),
                pltpu.SemaphoreType.DMA((2,)),
                pltpu.SemaphoreType.DMA,
                pltpu.SemaphoreType.DMA,
            ],
        ),
        out_shape=jax.ShapeDtypeStruct((n_rows, d), F32),
        compiler_params=_cparams("arbitrary"),
        name="moe_dispatch",
    )(end_p, dest_tiles, h.reshape(t // s, s, d))
    return xs


FFN_SPLIT_COLS = 512


def _ffn_kernel(be_ref, nb_ref, xs_ref, wgu_ref, bg_ref, bu_ref, wd_ref, bd_ref, y_ref, wg_sc, wu_sc, wd_sc):
    i = pl.program_id(0)

    @pl.when(i < nb_ref[0])
    def _():
        @pl.when(jnp.logical_or(i == 0, be_ref[i] != be_ref[jnp.maximum(i - 1, 0)]))
        def _():
            cols = FFN_SPLIT_COLS
            r = lax.broadcasted_iota(jnp.int32, (cols, cols), 0)
            c = lax.broadcasted_iota(jnp.int32, (cols, cols), 1)
            src = jnp.where(c < cols // 2, 2 * c, 2 * (c - cols // 2) + 1)
            perm = jnp.where(r == src, 1.0, 0.0).astype(BF16)
            for j in range(wgu_ref.shape[3] // cols):
                blk = wgu_ref[0, 0, :, j * cols:(j + 1) * cols].astype(BF16)
                out = jnp.dot(blk, perm, preferred_element_type=F32).astype(BF16)
                wg_sc[:, j * (cols // 2):(j + 1) * (cols // 2)] = out[:, :cols // 2]
                wu_sc[:, j * (cols // 2):(j + 1) * (cols // 2)] = out[:, cols // 2:]
            wd_sc[...] = wd_ref[0, 0].astype(BF16)

        xb = xs_ref[...].astype(BF16)
        g = jnp.dot(xb, wg_sc[...], preferred_element_type=F32) + bg_ref[0]
        u = jnp.dot(xb, wu_sc[...], preferred_element_type=F32) + bu_ref[0]
        g = jnp.minimum(g, SWIGLU_LIMIT)
        u = jnp.clip(u, -SWIGLU_LIMIT, SWIGLU_LIMIT)
        a = (u + 1.0) * (g * jax.nn.sigmoid(SWIGLU_ALPHA * g))
        y = jnp.dot(a.astype(BF16), wd_sc[...], preferred_element_type=F32) + bd_ref[0]
        for s in range(SUBLANES):
            y_ref[pl.ds(s, y.shape[0], stride=SUBLANES), :] = y[:, s * 128:(s + 1) * 128]

    @pl.when(i >= nb_ref[0])
    def _():
        y_ref[...] = jnp.zeros(y_ref.shape, F32)


def _expert_ffn(block_e, n_used, xs, w, rows):
    n_rows, d = xs.shape
    f2 = w["w_gu"].shape[3]
    f = f2 // 2
    assert f2 % FFN_SPLIT_COLS == 0
    layer = w["layer"]
    ex = lambda i, be, nb: (be[i], 0, 0)
    lex = lambda i, be, nb: (layer, be[i], 0, 0)
    return pl.pallas_call(
        _ffn_kernel,
        grid_spec=pltpu.PrefetchScalarGridSpec(
            num_scalar_prefetch=2,
            grid=(n_rows // rows,),
            in_specs=[
                pl.BlockSpec((rows, d), lambda i, be, nb: (jnp.minimum(i, nb[0] - 1), 0)),
                pl.BlockSpec((1, 1, d, f2), lex),
                pl.BlockSpec((1, 1, f), ex),
                pl.BlockSpec((1, 1, f), ex),
                pl.BlockSpec((1, 1, f, d), lex),
                pl.BlockSpec((1, 1, d), ex),
            ],
            out_specs=pl.BlockSpec((rows * SUBLANES, d // SUBLANES), lambda i, be, nb: (i, 0)),
            scratch_shapes=[pltpu.VMEM((d, f), BF16), pltpu.VMEM((d, f), BF16), pltpu.VMEM((f, d), BF16)],
        ),
        out_shape=jax.ShapeDtypeStruct((n_rows * SUBLANES, d // SUBLANES), F32),
        compiler_params=_cparams("arbitrary"),
        name="moe_ffn",
    )(block_e, n_used, xs, w["w_gu"], w["b_g"], w["b_u"], w["w_d"], w["b_d"])


def _combine_kernel(dest_hbm, y_hbm, x_ref, gate_ref, mod_ref, fn_ref, o_ref, dsm, yg, isem, sem, *, tile, final):
    base = _fetch_rows_of_step(dest_hbm, dsm, isem, TOP_K * tile)

    def issue(grp, carry):
        for u in range(SUBLANES):
            for k in range(TOP_K):
                row = dsm[base + (grp * SUBLANES + u) * TOP_K + k]
                src = y_hbm.at[pl.ds(pl.multiple_of(row * SUBLANES, SUBLANES), SUBLANES), :]
                dst = yg.at[k, pl.ds(pl.multiple_of((grp * SUBLANES + u) * SUBLANES, SUBLANES), SUBLANES), :]
                pltpu.make_async_copy(src, dst, sem).start(priority=k % 2)
        return carry

    lax.fori_loop(0, tile // SUBLANES, issue, 0)
    for k in range(TOP_K):
        pltpu.make_async_copy(yg.at[k], yg.at[k], sem).wait()
    gate = gate_ref[...]
    d = x_ref.shape[1]
    def rows_of(k):
        return jnp.concatenate([yg[k, pl.ds(s, tile, stride=SUBLANES), :] for s in range(SUBLANES)], axis=1)

    acc = rows_of(0) * gate[:, 0:1]
    for k in range(1, TOP_K):
        acc = acc + rows_of(k) * gate[:, k:k + 1]
    xn = x_ref[...] + _modrow(mod_ref, 5) * acc
    o_ref[...] = _rms(xn, fn_ref[...]) if final else xn


def _combine(dest_tiles, y, x, gate_t, mod, mod_row, final_norm, tile, final):
    t, d = x.shape
    s = SUBLANES
    return pl.pallas_call(
        functools.partial(_combine_kernel, tile=tile, final=final),
        grid=(t // tile,),
        in_specs=[
            pl.BlockSpec(memory_space=pl.ANY),
            pl.BlockSpec(memory_space=pl.ANY),
            pl.BlockSpec((tile, d), lambda i: (i, 0)),
            pl.BlockSpec((tile, TOP_K), lambda i: (i, 0)),
            pl.BlockSpec((1, 6, d), lambda i: (mod_row(i), 0, 0)),
            pl.BlockSpec((1, d), lambda i: (0, 0)),
        ],
        out_specs=pl.BlockSpec((tile, d), lambda i: (i, 0)),
        out_shape=jax.ShapeDtypeStruct((t, d), F32),
        scratch_shapes=[
            pltpu.SMEM((2 * TOP_K * tile,), jnp.int32),
            pltpu.VMEM((TOP_K, tile * s, d // s), F32),
            pltpu.SemaphoreType.DMA((2,)),
            pltpu.SemaphoreType.DMA,
        ],
        compiler_params=_cparams("arbitrary"),
        name="moe_combine",
    )(dest_tiles, y, x, gate_t, mod, final_norm)


def _moe(h, x, mod, mod_row_of_token, w, final_norm, final):
    t, d = h.shape
    rows = MOE_ROWS
    tile = min(COPY_TILE, t)
    assert t % tile == 0 and (TOP_K * tile) % 1024 == 0
    idx, gate, rank, counts = _route(h, w["w_router"], w["b_router"], ROUTER_TILE)
    padded = (counts + rows - 1) // rows * rows
    end_p = jnp.cumsum(padded)
    start_p = end_p - padded
    experts = jnp.arange(N_EXPERTS, dtype=jnp.int32)
    dest = jnp.sum(jnp.where(idx[:, :, None] == experts, start_p.astype(jnp.int32), 0), axis=-1) + rank
    n_rows = t * TOP_K + N_EXPERTS * rows
    n_blocks = n_rows // rows
    block_start = jnp.arange(n_blocks, dtype=jnp.int32) * rows
    block_e = jnp.minimum(jnp.sum(end_p[None, :] <= block_start[:, None], axis=1), N_EXPERTS - 1).astype(jnp.int32)
    n_used = (end_p[-1] // rows).astype(jnp.int32).reshape(1)
    dest_tiles = dest.T.reshape(-1)
    xs = _dispatch(end_p.astype(jnp.int32), dest_tiles, h, n_rows, tile, rows)
    y = _expert_ffn(block_e, n_used, xs, w, rows)
    return _combine(dest_tiles, y, x, gate.T, mod, lambda i: mod_row_of_token(i * tile), final_norm, tile, final)


def _log_sigmoid(x):
    return jnp.minimum(x, 0.0) - jnp.log1p(jnp.exp(-jnp.abs(x)))


def _gla_proj_kernel(x_ref, mod_ref, n1_ref, win_ref, wg1_ref, wg2_ref, bg_ref, q_ref, k_ref, v_ref, r_ref, g_ref,
                     *, nqk, nv, qscale):
    h = _rms(x_ref[...], n1_ref[...]) * (1.0 + _modrow(mod_ref, 1)) + _modrow(mod_ref, 0)
    hb = h.astype(BF16)
    a = jnp.dot(hb, win_ref[...], preferred_element_type=F32)
    q_ref[...] = a[:, :nqk] * qscale
    k_ref[...] = a[:, nqk:2 * nqk]
    v_ref[...] = a[:, 2 * nqk:2 * nqk + nv].astype(BF16)
    r_ref[...] = a[:, 2 * nqk + nv:]
    lo = jnp.dot(hb, wg1_ref[...], preferred_element_type=F32)
    lo_hi = lo.astype(BF16)
    lo_lo = (lo - lo_hi.astype(F32)).astype(BF16)
    z = jnp.dot(jnp.concatenate([lo_hi, lo_lo, lo_hi], axis=1), wg2_ref[...], preferred_element_type=F32) + bg_ref[...]
    for dr in range(2):
        g_ref[dr] = _log_sigmoid(z[:, dr * nqk:(dr + 1) * nqk]) * float(np.log2(np.e) / GLA_GATE_NORM)


def _gla_project(x, mod, mod_row_of_token, n1, w, tile):
    t, d = x.shape
    tile = min(tile, t)
    assert t % tile == 0
    nqk = w["w_gate2"].shape[1] // 2
    nv = (w["w_in"].shape[1] - 2 * nqk) // 2
    dk = nqk // GLA_HEADS
    const = lambda *shape: pl.BlockSpec(shape, lambda i: (0,) * len(shape))
    row = lambda n: pl.BlockSpec((tile, n), lambda i: (i, 0))
    return pl.pallas_call(
        functools.partial(_gla_proj_kernel, nqk=nqk, nv=nv, qscale=float(dk ** -0.5)),
        grid=(t // tile,),
        in_specs=[
            row(d),
            pl.BlockSpec((1, 6, d), lambda i: (mod_row_of_token(i * tile), 0, 0)),
            const(1, d),
            const(*w["w_in"].shape),
            const(*w["w_gate1"].shape),
            const(*w["w_gate2"].shape),
            const(*w["b_gate"].shape),
        ],
        out_specs=[row(nqk), row(nqk), row(nv), row(nv), pl.BlockSpec((2, tile, nqk), lambda i: (0, i, 0))],
        out_shape=[
            jax.ShapeDtypeStruct((t, nqk), F32),
            jax.ShapeDtypeStruct((t, nqk), F32),
            jax.ShapeDtypeStruct((t, nv), BF16),
            jax.ShapeDtypeStruct((t, nv), F32),
            jax.ShapeDtypeStruct((2, t, nqk), F32),
        ],
        compiler_params=_cparams("parallel"),
        name="gla_project",
    )(x, mod, n1, w["w_in"], w["w_gate1"], w["w_gate2"], w["b_gate"])


def _gla_level_tables(chunk):
    n_lvl = int(np.log2(chunk))
    i = np.arange(chunk)[:, None]
    j = np.arange(chunk)[None, :]
    level = np.full((chunk, chunk), -1, np.int32)
    for lv in range(n_lvl):
        s = chunk >> (lv + 1)
        same = (i // (2 * s)) == (j // (2 * s))
        level[same & ((i % (2 * s)) >= s) & ((j % (2 * s)) < s)] = lv
    level[i == j] = n_lvl
    lower = (j <= i)
    cum = np.stack([lower, lower.T]).astype(np.float32)
    return np.stack([level, level.T]), np.concatenate([cum, cum], axis=2)


def _ref_rows(b, s):
    c, n = b.shape
    if s >= 4:
        return jnp.broadcast_to(b.reshape(c // (2 * s), 2 * s, n)[:, s:s + 1, :], (c // (2 * s), 2 * s, n)).reshape(c, n)
    pos = lax.broadcasted_iota(jnp.int32, (c, n), 0) % (2 * s)
    out = b
    for p in range(2 * s):
        if p != s:
            out = jnp.where(pos == p, pltpu.roll(b, (p - s) % c, axis=0), out)
    return out


def _gla_scan_kernel(qf_ref, kf_ref, vf_ref, gf_ref, qb_ref, kb_ref, vb_ref, gb_ref, lvl_ref, cum_ref,
                     of_ref, ob_ref, *st_sc, chunk, dk, dv, nch):
    @pl.when(pl.program_id(1) == 0)
    def _():
        for st in st_sc:
            st[...] = jnp.zeros(st.shape, F32)

    n_lvl = int(np.log2(chunk))
    streams = ((qf_ref, kf_ref, vf_ref, gf_ref, of_ref), (qb_ref, kb_ref, vb_ref, gb_ref, ob_ref))

    lanes = [(dr, hh) for dr in range(2) for hh in range(GLA_HEADS)]
    cum = [cum_ref[dr].astype(BF16) for dr in range(2)]
    for ci in range(nch):
        qs, ks, vs, b, b_tot, rows = {}, {}, {}, {}, {}, {}
        for ln in lanes:
            dr, hh = ln
            q_ref, k_ref, v_ref, g_ref, _ = streams[dr]
            c_eff = ci if dr == 0 else nch - 1 - ci
            rows[ln] = slice(c_eff * chunk, (c_eff + 1) * chunk)
            qs[ln] = q_ref[rows[ln], hh * dk:(hh + 1) * dk]
            ks[ln] = k_ref[rows[ln], hh * dk:(hh + 1) * dk]
            vs[ln] = v_ref[rows[ln], hh * dv:(hh + 1) * dv]
            g = g_ref[0, rows[ln], hh * dk:(hh + 1) * dk]
            g_hi = g.astype(BF16)
            g_lo = (g - g_hi.astype(F32)).astype(BF16)
            b[ln] = jnp.dot(cum[dr], jnp.concatenate([g_hi, g_lo], axis=0), preferred_element_type=F32)
            b_tot[ln] = jnp.sum(g, axis=0, keepdims=True)
        qb = {ln: qs[ln].astype(BF16) for ln in lanes}
        kb = {ln: ks[ln].astype(BF16) for ln in lanes}
        a = {ln: jnp.where(lvl_ref[ln[0]] == n_lvl, lax.dot_general(qb[ln], kb[ln], _NT, preferred_element_type=F32), 0.0)
             for ln in lanes}
        for lv in range(n_lvl):
            for ln in lanes:
                e = jnp.exp2(-jnp.abs(b[ln] - _ref_rows(b[ln], chunk >> (lv + 1)))).astype(BF16)
                al = lax.dot_general(qb[ln] * e, kb[ln] * e, _NT, preferred_element_type=F32)
                a[ln] = jnp.where(lvl_ref[ln[0]] == lv, al, a[ln])
        for sidx, ln in enumerate(lanes):
            dr, hh = ln
            st = st_sc[sidx][...]
            q_in = (qs[ln] * jnp.exp2(b[ln])).astype(BF16)
            o = lax.dot_general(q_in, st.astype(BF16), _NT, preferred_element_type=F32)
            o = o + jnp.dot(a[ln].astype(BF16), vs[ln], preferred_element_type=F32)
            streams[dr][4][rows[ln], hh * dv:(hh + 1) * dv] = o
            k_out = (ks[ln] * jnp.exp2(b_tot[ln] - b[ln])).astype(BF16)
            st_sc[sidx][...] = st * jnp.exp2(b_tot[ln]) + lax.dot_general(vs[ln], k_out, _TN, preferred_element_type=F32)


def _gla_scan(q, k, v, g, batch, n_lat, n_ctx):
    t, nqk = q.shape
    nv = v.shape[1]
    blk = GLA_BLOCK
    assert n_ctx == blk and n_lat % blk == 0
    nlat = n_lat // blk
    ctx0 = batch * nlat
    chunk = GLA_CHUNK
    lvl, cum = _gla_level_tables(chunk)

    def rowblk(dr):
        def index(bb, j):
            lat = bb * nlat + (j - 1 if dr == 0 else nlat - j)
            return jnp.where(j == 0, ctx0 + bb, lat)
        return index

    def tok(n, dr):
        return pl.BlockSpec((blk, n), lambda bb, j: (rowblk(dr)(bb, j), 0))

    def gate(dr):
        return pl.BlockSpec((1, blk, nqk), lambda bb, j: (dr, rowblk(dr)(bb, j), 0))

    const = lambda *shape: pl.BlockSpec(shape, lambda bb, j: (0,) * len(shape))
    return pl.pallas_call(
        functools.partial(_gla_scan_kernel, chunk=chunk, dk=nqk // GLA_HEADS, dv=nv // GLA_HEADS, nch=blk // chunk),
        grid=(batch, nlat + 1),
        in_specs=[
            tok(nqk, 0), tok(nqk, 0), tok(nv, 0), gate(0),
            tok(nqk, 1), tok(nqk, 1), tok(nv, 1), gate(1),
            const(2, chunk, chunk),
            const(2, chunk, 2 * chunk),
        ],
        out_specs=[tok(nv, 0), tok(nv, 1)],
        out_shape=[jax.ShapeDtypeStruct((t, nv), F32)] * 2,
        scratch_shapes=[pltpu.VMEM((nv // GLA_HEADS, nqk // GLA_HEADS), F32)] * (2 * GLA_HEADS),
        compiler_params=_cparams("parallel", "arbitrary"),
        name="gla_scan",
    )(q, k, v, g, q, k, v, g, jnp.asarray(lvl), jnp.asarray(cum))


def _gla_finish_kernel(of_ref, ob_ref, r_ref, x_ref, mod_ref, hn_ref, wo_ref, n2_ref, xo_ref, h2_ref, *, dv):
    o = of_ref[...] + ob_ref[...]
    r = r_ref[...]
    parts = []
    for hh in range(GLA_HEADS):
        sl = slice(hh * dv, (hh + 1) * dv)
        parts.append(_rms(o[:, sl], hn_ref[...]) * (r[:, sl] * jax.nn.sigmoid(r[:, sl])))
    gated = jnp.concatenate(parts, axis=-1).astype(BF16)
    y = jnp.dot(gated, wo_ref[...], preferred_element_type=F32)
    xn = x_ref[...] + _modrow(mod_ref, 2) * y
    xo_ref[...] = xn
    h2_ref[...] = _rms(xn, n2_ref[...]) * (1.0 + _modrow(mod_ref, 4)) + _modrow(mod_ref, 3)


def _gla_finish(o_f, o_b, r, x, t, mod, mod_row_of_token, w, n2, tile):
    d = x.shape[1]
    nv = r.shape[1]
    tile = min(tile, t)
    assert t % tile == 0
    const = lambda *shape: pl.BlockSpec(shape, lambda i: (0,) * len(shape))
    row = lambda n: pl.BlockSpec((tile, n), lambda i: (i, 0))
    return pl.pallas_call(
        functools.partial(_gla_finish_kernel, dv=nv // GLA_HEADS),
        grid=(t // tile,),
        in_specs=[
            row(nv), row(nv), row(nv), row(d),
            pl.BlockSpec((1, 6, d), lambda i: (mod_row_of_token(i * tile), 0, 0)),
            const(1, nv // GLA_HEADS),
            const(nv, d),
            const(1, d),
        ],
        out_specs=[row(d), row(d)],
        out_shape=[jax.ShapeDtypeStruct((t, d), F32), jax.ShapeDtypeStruct((t, d), F32)],
        compiler_params=_cparams("parallel"),
        name="gla_finish",
    )(o_f, o_b, r, x, mod, w["head_norm"], w["w_o"], n2)


def _rope_tables(n):
    rows = n // GRID_W
    row = jnp.repeat(jnp.arange(rows, dtype=F32), GRID_W)
    col = jnp.tile(jnp.arange(GRID_W, dtype=F32), rows)
    nf = MLA_ROPE // 4
    inv = jnp.power(ROPE_BASE, -jnp.arange(nf, dtype=F32) / nf)
    ang = jnp.concatenate([row[:, None] * inv, col[:, None] * inv], axis=-1)
    cos, sin = jnp.cos(ang), jnp.sin(ang)
    cr, cc, sr, sc = cos[:, :nf], cos[:, nf:], sin[:, :nf], sin[:, nf:]
    cos64 = jnp.concatenate([cr, cr, cc, cc], axis=-1)
    sin64 = jnp.concatenate([-sr, sr, -sc, sc], axis=-1)
    return jnp.tile(cos64, (1, MLA_HEADS)), jnp.tile(sin64, (1, MLA_HEADS))


def _swap_halves(w):
    nf = MLA_ROPE // 4
    perm = np.concatenate([np.arange(nf, 2 * nf), np.arange(0, nf), np.arange(3 * nf, 4 * nf), np.arange(2 * nf, 3 * nf)])
    return w[..., perm]


def _mla_weights(w_in, q_norm, w_uq, kv_norm, w_ukv, w_o):
    hh = MLA_HEADS
    o = MLA_Q_RANK + MLA_KV_RANK
    w_in_ext = jnp.concatenate([w_in, _swap_halves(w_in[:, o:])], axis=1)
    uq = w_uq.reshape(MLA_Q_RANK, hh, MLA_NOPE + MLA_ROPE)
    w_uk = w_ukv[:, :hh * MLA_NOPE].reshape(MLA_KV_RANK, hh, MLA_NOPE)
    w_uv = w_ukv[:, hh * MLA_NOPE:].reshape(MLA_KV_RANK, hh, MLA_V)
    return {
        "w_in": w_in_ext.astype(BF16),
        "q_norm": q_norm.reshape(1, -1),
        "kv_norm": kv_norm.reshape(1, -1),
        "w_qn": uq[:, :, :MLA_NOPE].reshape(MLA_Q_RANK, hh * MLA_NOPE).astype(BF16),
        "w_qr": uq[:, :, MLA_NOPE:].reshape(MLA_Q_RANK, hh * MLA_ROPE).astype(BF16),
        "w_qs": _swap_halves(uq[:, :, MLA_NOPE:]).reshape(MLA_Q_RANK, hh * MLA_ROPE).astype(BF16),
        "w_ukT": jnp.transpose(w_uk, (1, 2, 0)).astype(BF16),
        "w_uv": jnp.transpose(w_uv, (1, 0, 2)).astype(BF16),
        "w_o": w_o.astype(BF16),
    }


def _split_gate2(w2):
    z = jnp.zeros_like(w2[0])
    bd = jnp.concatenate([jnp.concatenate([w2[0], z], axis=1), jnp.concatenate([z, w2[1]], axis=1)], axis=0)
    hi = bd.astype(BF16)
    lo = (bd - hi.astype(F32)).astype(BF16)
    return jnp.concatenate([hi, hi, lo], axis=0)


def _moe_weights(layer, w_router, b_router, w_gu, b_gu, w_down, b_down):
    e = w_gu.shape[1]
    return {
        "layer": layer,
        "w_router": w_router[layer],
        "b_router": b_router[layer],
        "w_gu": w_gu,
        "b_g": b_gu[layer, :, 0::2].reshape(e, 1, -1),
        "b_u": b_gu[layer, :, 1::2].reshape(e, 1, -1),
        "w_d": w_down,
        "b_d": b_down[layer].reshape(e, 1, -1),
    }


def kernel(x, c, ctx, c_ctx, mod_w, mod_b, norm1, norm2, mla_w_in, mla_q_norm, mla_w_uq, mla_kv_norm, mla_w_ukv, mla_w_o, gla_w_in, gla_w_gate1, gla_w_gate2, gla_b_gate, gla_head_norm, gla_w_o, moe_w_router, moe_b_router, moe_w_gu, moe_b_gu, moe_w_down, moe_b_down, final_norm):
    B, N, D = x.shape
    Lc = ctx.shape[1]
    assert B < 8
    cc = jnp.zeros((8, D), F32).at[:B].set(c).at[B].set(c_ctx)
    mod = _modulation(cc, mod_w, mod_b).reshape(mod_w.shape[0], 8, 6, D)
    row_of_token = lambda tok: jnp.minimum(tok // N, B)
    n1 = norm1.reshape(-1, 1, D)
    n2 = norm2.reshape(-1, 1, D)
    fnorm = final_norm.reshape(1, D)

    mw = _mla_weights(mla_w_in[0], mla_q_norm[0], mla_w_uq[0], mla_kv_norm[0], mla_w_ukv[0], mla_w_o[0])
    cos, sin = _rope_tables(N)
    ones = jnp.ones((Lc, cos.shape[1]), F32)
    q_l, k_l = _mla_project(x, mod[0], lambda b: b, n1[0], mw, cos, sin, TOKEN_TILE)
    q_c, k_c = _mla_project(ctx, mod[0], lambda b: B, n1[0], mw, ones, jnp.zeros_like(ones), TOKEN_TILE)
    k_all = jnp.concatenate([k_l, k_c], axis=1)
    x1, h1 = _mla_attend(q_l, k_all, x, mod[0], lambda b: b, n2[0], mw, ATTN_Q_TILE, ATTN_KV_TILE)
    xc1, hc1 = _mla_attend(q_c, k_c, ctx, mod[0], lambda b: B, n2[0], mw, ATTN_Q_TILE, ATTN_KV_TILE)
    x_all = jnp.concatenate([x1, xc1], axis=0)
    h_all = jnp.concatenate([h1, hc1], axis=0)
    ew = _moe_weights(0, moe_w_router, moe_b_router, moe_w_gu, moe_b_gu, moe_w_down, moe_b_down)
    x_all = _moe(h_all, x_all, mod[0], row_of_token, ew, fnorm, final=False)

    gw = {
        "w_in": gla_w_in[0].astype(BF16),
        "w_gate1": jnp.concatenate([gla_w_gate1[0, 0], gla_w_gate1[0, 1]], axis=1).astype(BF16),
        "w_gate2": _split_gate2(gla_w_gate2[0]),
        "b_gate": gla_b_gate[0].reshape(1, -1),
        "head_norm": gla_head_norm[0].reshape(1, -1),
        "w_o": gla_w_o[0].astype(BF16),
    }
    q, k, v, r, g = _gla_project(x_all, mod[1], row_of_token, n1[1], gw, TOKEN_TILE)
    o_f, o_b = _gla_scan(q, k, v, g, B, N, Lc)
    x2, h2 = _gla_finish(o_f, o_b, r, x_all, B * N, mod[1], row_of_token, gw, n2[1], TOKEN_TILE)
    ew = _moe_weights(1, moe_w_router, moe_b_router, moe_w_gu, moe_b_gu, moe_w_down, moe_b_down)
    out = _moe(h2, x2, mod[1], row_of_token, ew, fnorm, final=True)
    return out.reshape(B, N, D)
```
